```python
import math
import jax, jax.numpy as jnp
from jax import lax
import numpy as np

D_MODEL = 1024
BATCH = 2
SEQ = 16384
DEPTH = 2

N_HEADS = 16
N_KV_HEADS = 2
GROUP = N_HEADS // N_KV_HEADS
HEAD_DIM = 64
WINDOW = 128
BLOCK = 128
CONV_WIDTH = 31
D_FF = 2816
FFN_CONV_WIDTH = 3
RMS_EPS = 1e-6
LN_EPS = 1e-5
N_MIXERS = 2
N_ATTN = (DEPTH + 1) // 2
N_CONV = DEPTH // 2

kernel_name = "hybrid_swa_sink_conformer_convffn"


def rmsnorm(x, g):
    xf = x.astype(jnp.float32)
    y = xf * lax.rsqrt(jnp.mean(xf * xf, axis=-1, keepdims=True) + RMS_EPS)
    return (y * g.astype(jnp.float32)).astype(x.dtype)


def layernorm(x, g, b):
    xf = x.astype(jnp.float32)
    mu = jnp.mean(xf, axis=-1, keepdims=True)
    var = jnp.mean(jnp.square(xf - mu), axis=-1, keepdims=True)
    y = (xf - mu) * lax.rsqrt(var + LN_EPS)
    return (y * g.astype(jnp.float32) + b.astype(jnp.float32)).astype(x.dtype)


def causal_dwconv(x, w, b):
    k, c = w.shape
    y = lax.conv_general_dilated(
        x, w[:, None, :].astype(x.dtype), window_strides=(1,),
        padding=((k - 1, 0),), dimension_numbers=("NWC", "WIO", "NWC"),
        feature_group_count=c)
    return y + b.astype(x.dtype)


def sliding_window_sink_attention(h, w_qkv, b_qkv, sinks, w_o, b_o):
    bsz, seq, _ = h.shape
    nb = seq // BLOCK
    qkv = h @ w_qkv + b_qkv
    q_dim = N_HEADS * HEAD_DIM
    kv_dim = N_KV_HEADS * HEAD_DIM
    q = qkv[..., :q_dim].reshape(bsz, nb, BLOCK, N_KV_HEADS, GROUP, HEAD_DIM)
    k = qkv[..., q_dim:q_dim + kv_dim].reshape(bsz, seq, N_KV_HEADS, HEAD_DIM)
    v = qkv[..., q_dim + kv_dim:].reshape(bsz, seq, N_KV_HEADS, HEAD_DIM)

    def band_keys(t):
        tp = jnp.pad(t, ((0, 0), (BLOCK, 0), (0, 0), (0, 0)))
        tp = tp.reshape(bsz, nb + 1, BLOCK, N_KV_HEADS, HEAD_DIM)
        return jnp.concatenate([tp[:, :-1], tp[:, 1:]], axis=2)

    kb = band_keys(k)
    vb = band_keys(v)
    scale = 1.0 / math.sqrt(HEAD_DIM)
    s = jnp.einsum("bnqhgd,bnkhd->bnhgqk", q, kb).astype(jnp.float32) * scale

    qi = jnp.arange(BLOCK)[:, None]
    ki = jnp.arange(2 * BLOCK)[None, :]
    dist = qi + BLOCK - ki
    band = (dist >= 0) & (dist < WINDOW)
    first = (jnp.arange(nb) == 0)[:, None, None] & (ki < BLOCK)[None]
    valid = band[None] & ~first
    s = jnp.where(valid[None, :, None, None], s, jnp.finfo(jnp.float32).min)

    sink = sinks.astype(jnp.float32).reshape(1, 1, N_KV_HEADS, GROUP, 1, 1)
    m = jnp.maximum(jnp.max(s, axis=-1, keepdims=True), sink)
    p = jnp.exp(s - m)
    denom = jnp.sum(p, axis=-1, keepdims=True) + jnp.exp(sink - m)
    probs = (p / denom).astype(v.dtype)
    o = jnp.einsum("bnhgqk,bnkhd->bnqhgd", probs, vb)
    o = o.reshape(bsz, seq, q_dim)
    return o @ w_o + b_o


def conformer_conv_module(h, w_pw1, b_pw1, w_dw, b_dw, ln_g, ln_b, w_pw2, b_pw2):
    a = h @ w_pw1 + b_pw1
    u = a[..., :D_MODEL] * jax.nn.sigmoid(a[..., D_MODEL:])
    u = causal_dwconv(u, w_dw, b_dw)
    u = jax.nn.silu(layernorm(u, ln_g, ln_b))
    return u @ w_pw2 + b_pw2


def conv_ffn(h, w_up, w_dw, b_dw, w_down):
    up = h @ w_up
    gate = causal_dwconv(up[..., :D_FF], w_dw, b_dw)
    return (jax.nn.silu(gate) * up[..., D_FF:]) @ w_down


def setup_inputs(seed: int = 0) -> dict:
    key = jax.random.key(seed)
    ks = jax.random.split(key, 24)
    f32 = jnp.float32
    qkv_dim = (N_HEADS + 2 * N_KV_HEADS) * HEAD_DIM

    def nrm(k, shape, scale):
        return jax.random.normal(k, shape, f32) * scale

    def gain(k, shape):
        return 1.0 + 0.02 * jax.random.normal(k, shape, f32)

    return {
        "x": jax.random.normal(ks[0], (BATCH, SEQ, D_MODEL), f32),
        "norm_mix": gain(ks[1], (DEPTH, D_MODEL)),
        "attn_w_qkv": nrm(ks[2], (N_ATTN, D_MODEL, qkv_dim), D_MODEL ** -0.5),
        "attn_b_qkv": nrm(ks[3], (N_ATTN, qkv_dim), 0.02),
        "attn_sinks": nrm(ks[4], (N_ATTN, N_HEADS), 0.5),
        "attn_w_o": nrm(ks[5], (N_ATTN, N_HEADS * HEAD_DIM, D_MODEL), (N_HEADS * HEAD_DIM) ** -0.5),
        "attn_b_o": nrm(ks[6], (N_ATTN, D_MODEL), 0.02),
        "conv_w_pw1": nrm(ks[7], (N_CONV, D_MODEL, 2 * D_MODEL), D_MODEL ** -0.5),
        "conv_b_pw1": nrm(ks[8], (N_CONV, 2 * D_MODEL), 0.02),
        "conv_w_dw": nrm(ks[9], (N_CONV, CONV_WIDTH, D_MODEL), CONV_WIDTH ** -0.5),
        "conv_b_dw": nrm(ks[10], (N_CONV, D_MODEL), 0.02),
        "conv_ln_g": gain(ks[11], (N_CONV, D_MODEL)),
        "conv_ln_b": nrm(ks[12], (N_CONV, D_MODEL), 0.02),
        "conv_w_pw2": nrm(ks[13], (N_CONV, D_MODEL, D_MODEL), D_MODEL ** -0.5),
        "conv_b_pw2": nrm(ks[14], (N_CONV, D_MODEL), 0.02),
        "norm_ffn": gain(ks[15], (DEPTH, D_MODEL)),
        "ffn_w_up": nrm(ks[16], (DEPTH, D_MODEL, 2 * D_FF), D_MODEL ** -0.5),
        "ffn_w_dw": nrm(ks[17], (DEPTH, FFN_CONV_WIDTH, D_FF), FFN_CONV_WIDTH ** -0.5),
        "ffn_b_dw": nrm(ks[18], (DEPTH, D_FF), 0.02),
        "ffn_w_down": nrm(ks[19], (DEPTH, D_FF, D_MODEL), D_FF ** -0.5),
        "final_norm": gain(ks[20], (D_MODEL,)),
    }


def reference(x, norm_mix, attn_w_qkv, attn_b_qkv, attn_sinks, attn_w_o, attn_b_o,
              conv_w_pw1, conv_b_pw1, conv_w_dw, conv_b_dw, conv_ln_g, conv_ln_b,
              conv_w_pw2, conv_b_pw2, norm_ffn, ffn_w_up, ffn_w_dw, ffn_b_dw,
              ffn_w_down, final_norm):
    for i in range(DEPTH):
        h = rmsnorm(x, norm_mix[i])
        j = i // N_MIXERS
        if i % N_MIXERS == 0:
            x = x + sliding_window_sink_attention(
                h, attn_w_qkv[j], attn_b_qkv[j], attn_sinks[j], attn_w_o[j], attn_b_o[j])
        else:
            x = x + conformer_conv_module(
                h, conv_w_pw1[j], conv_b_pw1[j], conv_w_dw[j], conv_b_dw[j],
                conv_ln_g[j], conv_ln_b[j], conv_w_pw2[j], conv_b_pw2[j])
        h = rmsnorm(x, norm_ffn[i])
        x = x + conv_ffn(h, ffn_w_up[i], ffn_w_dw[i], ffn_b_dw[i], ffn_w_down[i])
    return rmsnorm(x, final_norm)
```

```python
import functools
import math

import jax
import jax.numpy as jnp
from jax import lax
from jax.experimental import pallas as pl
from jax.experimental.pallas import tpu as pltpu

D_MODEL = 1024
N_HEADS = 16
N_KV_HEADS = 2
GROUP = N_HEADS // N_KV_HEADS
HEAD_DIM = 64
WINDOW = 128
BLOCK = 128
CONV_WIDTH = 31
D_FF = 2816
FFN_CONV_WIDTH = 3
RMS_EPS = 1e-6
LN_EPS = 1e-5

Q_DIM = N_HEADS * HEAD_DIM
KV_DIM = N_KV_HEADS * HEAD_DIM
QKV_DIM = Q_DIM + 2 * KV_DIM

V7X_LANES = 128
V7X_SUBLANES = 8
V7X_MXU_DIM = 256
V7X_VMEM_LIMIT_BYTES = 56 * 1024 * 1024

HEADS_PER_QUAD = V7X_MXU_DIM // HEAD_DIM
N_QUADS = N_HEADS // HEADS_PER_QUAD
QUAD_ROWS = HEADS_PER_QUAD * BLOCK

TOKEN_TILE = 512
FFN_CHUNK = 256
CONV_ROWS = 64
CONV_PAD = 32

F32 = jnp.float32
BF16 = jnp.bfloat16
NEG_MIN = float(jnp.finfo(jnp.float32).min)


def _rmsnorm(xf, g):
    ms = jnp.mean(xf * xf, axis=-1, keepdims=True)
    return xf * lax.rsqrt(ms + RMS_EPS) * g


def _lane_block(shape, width):
    return lax.broadcasted_iota(jnp.int32, shape, len(shape) - 1) // width


def _attn_kernel(x_ref, g_ref, wqkv_ref, bqkv_ref, sink_ref, wo_ref, bo_ref,
                 o_ref, kv_ref, oacc_ref):
    tq = x_ref.shape[1]
    n_blocks = tq // BLOCK
    t_idx = pl.program_id(1)

    @pl.when(t_idx == 0)
    def _():
        kv_ref[0:BLOCK, :] = jnp.zeros((BLOCK, kv_ref.shape[1]), BF16)

    xf = x_ref[0]
    h = _rmsnorm(xf, g_ref[...]).astype(BF16)
    qkv = jnp.dot(h, wqkv_ref[...], preferred_element_type=F32) + bqkv_ref[...]

    qf = qkv[:, :Q_DIM] * (1.0 / math.sqrt(HEAD_DIM))

    def replicate(t):
        tr = pltpu.roll(t, HEAD_DIM, axis=1)
        first = _lane_block(t.shape, HEAD_DIM) == 0
        a = jnp.where(first, t, tr)
        b = jnp.where(first, tr, t)
        return jnp.concatenate([a, a, b, b], axis=1)

    krep = replicate(qkv[:, Q_DIM:Q_DIM + KV_DIM])
    vrep = replicate(qkv[:, Q_DIM + KV_DIM:])
    kv_ref[BLOCK:, :] = jnp.concatenate([krep, vrep], axis=1).astype(BF16)

    qi = lax.broadcasted_iota(jnp.int32, (QUAD_ROWS, 2 * BLOCK), 0) % BLOCK
    ki = lax.broadcasted_iota(jnp.int32, (QUAD_ROWS, 2 * BLOCK), 1)
    dist = qi + BLOCK - ki
    band = (dist >= 0) & (dist < WINDOW)
    lane_head = _lane_block((BLOCK, V7X_MXU_DIM), HEAD_DIM)
    ones = jnp.ones((2 * BLOCK, V7X_MXU_DIM), BF16)

    for j in range(n_blocks):
        if j == 0:
            valid = band & ((ki >= BLOCK) | (t_idx > 0))
        else:
            valid = band
        r0 = j * BLOCK
        for c in range(N_QUADS):
            hkv = c // (N_QUADS // N_KV_HEADS)
            kb = kv_ref[r0:r0 + 2 * BLOCK, hkv * 256:(hkv + 1) * 256]
            vb = kv_ref[r0:r0 + 2 * BLOCK, 512 + hkv * 256:512 + (hkv + 1) * 256]
            qq = qf[r0:r0 + BLOCK, c * 256:(c + 1) * 256]
            qexp = jnp.concatenate(
                [jnp.where(lane_head == i, qq, 0.0) for i in range(HEADS_PER_QUAD)],
                axis=0).astype(BF16)
            s = lax.dot_general(qexp, kb, (((1,), (1,)), ((), ())),
                                preferred_element_type=F32)
            s = jnp.where(valid, s, NEG_MIN)
            sink = sink_ref[c]
            m = jnp.maximum(jnp.max(s, axis=-1, keepdims=True), sink)
            p = jnp.exp(s - m).astype(BF16)
            sink_term = jnp.exp(sink - m)
            od = jnp.dot(p, jnp.concatenate([vb, ones], axis=1),
                         preferred_element_type=F32)
            on = od[:, :256] / (od[:, 256:] + sink_term)
            oq = jnp.where(lane_head == 0, on[0:BLOCK], 0.0)
            for i in range(1, HEADS_PER_QUAD):
                oq = jnp.where(lane_head == i, on[i * BLOCK:(i + 1) * BLOCK], oq)
            oacc_ref[r0:r0 + BLOCK, c * 256:(c + 1) * 256] = oq.astype(BF16)

    kv_ref[0:BLOCK, :] = kv_ref[tq:tq + BLOCK, :]
    out = jnp.dot(oacc_ref[...], wo_ref[...], preferred_element_type=F32)
    o_ref[0] = xf + out + bo_ref[...]


def _conv_kernel(x_ref, g_ref, w1_ref, b1_ref, wdw_ref, bdw_ref, lng_ref, lnb_ref,
                 w2_ref, b2_ref, o_ref, u_ref, y_ref):
    tc = x_ref.shape[1]
    n_lane = D_MODEL // V7X_LANES
    t_idx = pl.program_id(1)

    @pl.when(t_idx == 0)
    def _():
        u_ref[:, 0:CONV_PAD, :] = jnp.zeros((n_lane, CONV_PAD, V7X_LANES), F32)

    xf = x_ref[0]
    h = _rmsnorm(xf, g_ref[...]).astype(BF16)
    a = jnp.dot(h, w1_ref[...], preferred_element_type=F32) + b1_ref[...]
    u = a[:, :D_MODEL] * jax.nn.sigmoid(a[:, D_MODEL:])
    for l in range(n_lane):
        u_ref[l, CONV_PAD:, :] = u[:, l * V7X_LANES:(l + 1) * V7X_LANES]

    first_tap_row = CONV_PAD - (CONV_WIDTH - 1)
    n_row_chunks = tc // CONV_ROWS

    def conv_block(idx, carry):
        l = idx // n_row_chunks
        r0 = pl.multiple_of((idx % n_row_chunks) * CONV_ROWS, CONV_ROWS)
        acc = jnp.broadcast_to(bdw_ref[l], (CONV_ROWS, V7X_LANES))
        for j in range(CONV_WIDTH):
            win = u_ref[l, pl.ds(r0 + first_tap_row + j, CONV_ROWS), :]
            acc = acc + win * wdw_ref[l, j:j + 1, :]
        y_ref[l, pl.ds(r0, CONV_ROWS), :] = acc
        return carry

    lax.fori_loop(0, n_lane * n_row_chunks, conv_block, 0)

    for l in range(n_lane):
        u_ref[l, 0:CONV_PAD, :] = u_ref[l, tc:tc + CONV_PAD, :]

    y = jnp.concatenate([y_ref[l] for l in range(n_lane)], axis=1)
    mu = jnp.mean(y, axis=-1, keepdims=True)
    yc = y - mu
    var = jnp.mean(yc * yc, axis=-1, keepdims=True)
    z = yc * lax.rsqrt(var + LN_EPS) * lng_ref[...] + lnb_ref[...]
    z = (z * jax.nn.sigmoid(z)).astype(BF16)
    out = jnp.dot(z, w2_ref[...], preferred_element_type=F32)
    o_ref[0] = xf + out + b2_ref[...]


def _ffn_kernel(x_ref, g_ref, wup_ref, wdw_ref, bdw_ref, wdown_ref, gfin_ref,
                o_ref, tail_ref, act_ref, *, final_norm):
    tm = x_ref.shape[1]
    t_idx = pl.program_id(1)

    @pl.when(t_idx == 0)
    def _():
        tail_ref[...] = jnp.zeros(tail_ref.shape, F32)

    xf = x_ref[0]
    h = _rmsnorm(xf, g_ref[...]).astype(BF16)
    for c in range(D_FF // FFN_CHUNK):
        lo = c * FFN_CHUNK
        hi = lo + FFN_CHUNK
        gate = jnp.dot(h, wup_ref[:, lo:hi], preferred_element_type=F32)
        up = jnp.dot(h, wup_ref[:, D_FF + lo:D_FF + hi], preferred_element_type=F32)
        ext = jnp.concatenate([tail_ref[:, lo:hi], gate], axis=0)
        tail_ref[:, lo:hi] = gate[tm - V7X_SUBLANES:, :]
        conv = (gate * wdw_ref[2:3, lo:hi]
                + ext[V7X_SUBLANES - 1:V7X_SUBLANES - 1 + tm] * wdw_ref[1:2, lo:hi]
                + ext[V7X_SUBLANES - 2:V7X_SUBLANES - 2 + tm] * wdw_ref[0:1, lo:hi]
                + bdw_ref[:, lo:hi])
        act_ref[:, lo:hi] = (conv * jax.nn.sigmoid(conv) * up).astype(BF16)
    out = xf + jnp.dot(act_ref[...], wdown_ref[...], preferred_element_type=F32)
    if final_norm:
        out = _rmsnorm(out, gfin_ref[...])
    o_ref[0] = out


def _resident(shape):
    zeros = (0,) * len(shape)
    return pl.BlockSpec(shape, lambda b, t: zeros, pipeline_mode=pl.Buffered(1))


def _token_spec(tile):
    return pl.BlockSpec((1, tile, D_MODEL), lambda b, t: (b, t, 0))


_COMPILER_PARAMS = pltpu.CompilerParams(
    dimension_semantics=("arbitrary", "arbitrary"),
    vmem_limit_bytes=V7X_VMEM_LIMIT_BYTES,
)


def _attn_layer(x, g, wqkv, bqkv, sink_rows, wo, bo):
    bsz, seq, _ = x.shape
    tile = TOKEN_TILE
    return pl.pallas_call(
        _attn_kernel,
        out_shape=jax.ShapeDtypeStruct(x.shape, F32),
        grid=(bsz, seq // tile),
        in_specs=[
            _token_spec(tile),
            _resident(g.shape), _resident(wqkv.shape), _resident(bqkv.shape),
            _resident(sink_rows.shape), _resident(wo.shape), _resident(bo.shape),
        ],
        out_specs=_token_spec(tile),
        scratch_shapes=[
            pltpu.VMEM((BLOCK + tile, 2 * 2 * V7X_MXU_DIM), BF16),
            pltpu.VMEM((tile, Q_DIM), BF16),
        ],
        compiler_params=_COMPILER_PARAMS,
        name="swa_attention",
    )(x, g, wqkv, bqkv, sink_rows, wo, bo)


def _conv_layer(x, g, w1, b1, wdw, bdw, lng, lnb, w2, b2):
    bsz, seq, _ = x.shape
    tile = TOKEN_TILE
    n_lane = D_MODEL // V7X_LANES
    return pl.pallas_call(
        _conv_kernel,
        out_shape=jax.ShapeDtypeStruct(x.shape, F32),
        grid=(bsz, seq // tile),
        in_specs=[_token_spec(tile)] + [
            _resident(a.shape) for a in (g, w1, b1, wdw, bdw, lng, lnb, w2, b2)],
        out_specs=_token_spec(tile),
        scratch_shapes=[
            pltpu.VMEM((n_lane, CONV_PAD + tile, V7X_LANES), F32),
            pltpu.VMEM((n_lane, tile, V7X_LANES), F32),
        ],
        compiler_params=_COMPILER_PARAMS,
        name="conformer_conv",
    )(x, g, w1, b1, wdw, bdw, lng, lnb, w2, b2)


def _ffn_layer(x, g, wup, wdw, bdw, wdown, gfin, *, final_norm):
    bsz, seq, _ = x.shape
    tile = TOKEN_TILE
    return pl.pallas_call(
        functools.partial(_ffn_kernel, final_norm=final_norm),
        out_shape=jax.ShapeDtypeStruct(x.shape, F32),
        grid=(bsz, seq // tile),
        in_specs=[_token_spec(tile)] + [
            _resident(a.shape) for a in (g, wup, wdw, bdw, wdown, gfin)],
        out_specs=_token_spec(tile),
        scratch_shapes=[
            pltpu.VMEM((V7X_SUBLANES, D_FF), F32),
            pltpu.VMEM((tile, D_FF), BF16),
        ],
        compiler_params=_COMPILER_PARAMS,
        name="conv_ffn",
    )(x, g, wup, wdw, bdw, wdown, gfin)


def _row(v):
    return v.reshape(1, -1).astype(F32)


def kernel(x, norm_mix, attn_w_qkv, attn_b_qkv, attn_sinks, attn_w_o, attn_b_o,
           conv_w_pw1, conv_b_pw1, conv_w_dw, conv_b_dw, conv_ln_g, conv_ln_b,
           conv_w_pw2, conv_b_pw2, norm_ffn, ffn_w_up, ffn_w_dw, ffn_b_dw,
           ffn_w_down, final_norm):
    n_lane = D_MODEL // V7X_LANES
    gfin = _row(final_norm)

    sink_rows = jnp.repeat(attn_sinks[0].astype(F32).reshape(N_QUADS, HEADS_PER_QUAD),
                           BLOCK, axis=1).reshape(N_QUADS, QUAD_ROWS, 1)
    x = _attn_layer(x, _row(norm_mix[0]), attn_w_qkv[0].astype(BF16),
                    _row(attn_b_qkv[0]), sink_rows, attn_w_o[0].astype(BF16),
                    _row(attn_b_o[0]))
    x = _ffn_layer(x, _row(norm_ffn[0]), ffn_w_up[0].astype(BF16),
                   ffn_w_dw[0].astype(F32), _row(ffn_b_dw[0]),
                   ffn_w_down[0].astype(BF16), gfin, final_norm=False)

    wdw = jnp.pad(conv_w_dw[0].astype(F32), ((0, CONV_PAD - CONV_WIDTH), (0, 0)))
    wdw = wdw.reshape(CONV_PAD, n_lane, V7X_LANES).transpose(1, 0, 2)
    bdw = conv_b_dw[0].astype(F32).reshape(n_lane, 1, V7X_LANES)
    x = _conv_layer(x, _row(norm_mix[1]), conv_w_pw1[0].astype(BF16),
                    _row(conv_b_pw1[0]), wdw, bdw, _row(conv_ln_g[0]),
                    _row(conv_ln_b[0]), conv_w_pw2[0].astype(BF16),
                    _row(conv_b_pw2[0]))
    x = _ffn_layer(x, _row(norm_ffn[1]), ffn_w_up[1].astype(BF16),
                   ffn_w_dw[1].astype(F32), _row(ffn_b_dw[1]),
                   ffn_w_down[1].astype(BF16), gfin, final_norm=True)
    return x
```

```python
import functools
import math

import jax
import jax.numpy as jnp
from jax import lax
from jax.experimental import pallas as pl
from jax.experimental.pallas import tpu as pltpu

D_MODEL = 1024
N_HEADS = 16
N_KV_HEADS = 2
GROUP = N_HEADS // N_KV_HEADS
HEAD_DIM = 64
WINDOW = 128
BLOCK = 128
CONV_WIDTH = 31
D_FF = 2816
FFN_CONV_WIDTH = 3
RMS_EPS = 1e-6
LN_EPS = 1e-5

Q_DIM = N_HEADS * HEAD_DIM
KV_DIM = N_KV_HEADS * HEAD_DIM
QKV_DIM = Q_DIM + 2 * KV_DIM

V7X_LANES = 128
V7X_SUBLANES = 8
V7X_MXU_DIM = 256
V7X_VMEM_LIMIT_BYTES = 56 * 1024 * 1024

TOKEN_TILE = 512
ATTN_LOOKAHEAD = 2
FFN_CHUNK = 256
CONV_ROWS = 64
CONV_PAD = 32

F32 = jnp.float32
BF16 = jnp.bfloat16
NEG_MIN = float(jnp.finfo(jnp.float32).min)


def _rmsnorm(xf, g):
    ms = jnp.mean(xf * xf, axis=-1, keepdims=True)
    return xf * lax.rsqrt(ms + RMS_EPS) * g


def _lane_block(shape, width):
    return lax.broadcasted_iota(jnp.int32, shape, len(shape) - 1) // width


def _attn_kernel(sink_ref, x_ref, g_ref, wqvt_ref, bqvt_ref, wk_ref, bk_ref, wo_ref,
                 bo_ref, o_ref, k_ref, vt_ref, qt_ref, ot_ref):
    tq = x_ref.shape[1]
    n_blocks = tq // BLOCK
    t_idx = pl.program_id(1)
    nt_dims = (((1,), (1,)), ((), ()))
    tn_dims = (((0,), (0,)), ((), ()))

    @pl.when(t_idx == 0)
    def _():
        k_ref[0:BLOCK, :] = jnp.zeros((BLOCK, KV_DIM), BF16)
        vt_ref[:, 0:BLOCK] = jnp.zeros((KV_DIM, BLOCK), BF16)

    xf = x_ref[0]
    h = _rmsnorm(xf, g_ref[...]).astype(BF16)
    bias_t = jnp.concatenate([bqvt_ref[...]] * (tq // V7X_LANES), axis=1)
    qvt = lax.dot_general(wqvt_ref[...], h, nt_dims, preferred_element_type=F32) + bias_t
    qt_ref[...] = (qvt[:Q_DIM] * (1.0 / math.sqrt(HEAD_DIM))).astype(BF16)
    vt_ref[:, BLOCK:] = qvt[Q_DIM:].astype(BF16)
    kk = jnp.dot(h, wk_ref[...], preferred_element_type=F32) + bk_ref[...]
    k_ref[BLOCK:, :] = kk.astype(BF16)

    ki = lax.broadcasted_iota(jnp.int32, (2 * BLOCK, 2 * BLOCK), 0)
    qi = lax.broadcasted_iota(jnp.int32, (2 * BLOCK, 2 * BLOCK), 1) % BLOCK
    dist = qi + BLOCK - ki
    band = (dist >= 0) & (dist < WINDOW)
    second_head = lax.broadcasted_iota(jnp.int32, (1, 2 * BLOCK), 1) >= BLOCK
    zeros_qt = jnp.zeros((HEAD_DIM, BLOCK), BF16)
    ones_rows = jnp.ones((2 * V7X_SUBLANES, 2 * BLOCK), BF16)

    first_valid = band & ((ki >= BLOCK) | (t_idx > 0))

    def scores(j, head):
        c0 = j * BLOCK
        hkv = head // GROUP
        kb = k_ref[c0:c0 + 2 * BLOCK, :]
        cols = []
        for hd in (head, head + 1):
            qh = qt_ref[hd * HEAD_DIM:(hd + 1) * HEAD_DIM, c0:c0 + BLOCK]
            cols.append(jnp.concatenate(
                [qh, zeros_qt] if hkv == 0 else [zeros_qt, qh], axis=0))
        rhs = jnp.concatenate(cols, axis=1)
        return jnp.dot(kb, rhs, preferred_element_type=F32)

    def finish(j, head, st):
        c0 = j * BLOCK
        hkv = head // GROUP
        st = jnp.where(first_valid if j == 0 else band, st, NEG_MIN)
        sink = jnp.where(second_head, sink_ref[head + 1], sink_ref[head])
        m = jnp.maximum(jnp.max(st, axis=0, keepdims=True), sink)
        pt = jnp.exp(st - m).astype(BF16)
        vaug = jnp.concatenate(
            [vt_ref[hkv * HEAD_DIM:(hkv + 1) * HEAD_DIM, c0:c0 + 2 * BLOCK], ones_rows],
            axis=0)
        oa = jnp.dot(vaug, pt, preferred_element_type=F32)
        den = oa[HEAD_DIM:HEAD_DIM + 1, :] + jnp.exp(sink - m)
        ot = (oa[:HEAD_DIM, :] / den).astype(BF16)
        ot_ref[head * HEAD_DIM:(head + 1) * HEAD_DIM, c0:c0 + BLOCK] = ot[:, :BLOCK]
        ot_ref[(head + 1) * HEAD_DIM:(head + 2) * HEAD_DIM, c0:c0 + BLOCK] = ot[:, BLOCK:]

    items = [(j, head) for j in range(n_blocks) for head in range(0, N_HEADS, 2)]
    pending = [scores(*it) for it in items[:ATTN_LOOKAHEAD]]
    for n, it in enumerate(items):
        if n + ATTN_LOOKAHEAD < len(items):
            pending.append(scores(*items[n + ATTN_LOOKAHEAD]))
        finish(*it, pending.pop(0))

    k_ref[0:BLOCK, :] = k_ref[tq:tq + BLOCK, :]
    vt_ref[:, 0:BLOCK] = vt_ref[:, tq:tq + BLOCK]
    out = lax.dot_general(ot_ref[...], wo_ref[...], tn_dims, preferred_element_type=F32)
    o_ref[0] = xf + out + bo_ref[...]


def _conv_kernel(x_ref, g_ref, w1_ref, b1_ref, wdw_ref, bdw_ref, lng_ref, lnb_ref,
                 w2_ref, b2_ref, o_ref, u_ref, y_ref):
    tc = x_ref.shape[1]
    n_lane = D_MODEL // V7X_LANES
    t_idx = pl.program_id(1)

    @pl.when(t_idx == 0)
    def _():
        u_ref[:, 0:CONV_PAD, :] = jnp.zeros((n_lane, CONV_PAD, V7X_LANES), F32)

    xf = x_ref[0]
    h = _rmsnorm(xf, g_ref[...]).astype(BF16)
    a = jnp.dot(h, w1_ref[...], preferred_element_type=F32) + b1_ref[...]
    u = a[:, :D_MODEL] * jax.nn.sigmoid(a[:, D_MODEL:])
    for l in range(n_lane):
        u_ref[l, CONV_PAD:, :] = u[:, l * V7X_LANES:(l + 1) * V7X_LANES]

    first_tap_row = CONV_PAD - (CONV_WIDTH - 1)
    n_row_chunks = tc // CONV_ROWS

    def conv_block(idx, carry):
        l = idx // n_row_chunks
        r0 = pl.multiple_of((idx % n_row_chunks) * CONV_ROWS, CONV_ROWS)
        acc = jnp.broadcast_to(bdw_ref[l], (CONV_ROWS, V7X_LANES))
        for j in range(CONV_WIDTH):
            win = u_ref[l, pl.ds(r0 + first_tap_row + j, CONV_ROWS), :]
            acc = acc + win * wdw_ref[l, j:j + 1, :]
        y_ref[l, pl.ds(r0, CONV_ROWS), :] = acc
        return carry

    lax.fori_loop(0, n_lane * n_row_chunks, conv_block, 0)

    for l in range(n_lane):
        u_ref[l, 0:CONV_PAD, :] = u_ref[l, tc:tc + CONV_PAD, :]

    y = jnp.concatenate([y_ref[l] for l in range(n_lane)], axis=1)
    mu = jnp.mean(y, axis=-1, keepdims=True)
    yc = y - mu
    var = jnp.mean(yc * yc, axis=-1, keepdims=True)
    z = yc * lax.rsqrt(var + LN_EPS) * lng_ref[...] + lnb_ref[...]
    z = (z * jax.nn.sigmoid(z)).astype(BF16)
    out = jnp.dot(z, w2_ref[...], preferred_element_type=F32)
    o_ref[0] = xf + out + b2_ref[...]


def _ffn_kernel(x_ref, g_ref, wup_ref, wdw_ref, bdw_ref, wdown_ref, gfin_ref,
                o_ref, tail_ref, act_ref, *, final_norm):
    tm = x_ref.shape[1]
    t_idx = pl.program_id(1)

    @pl.when(t_idx == 0)
    def _():
        tail_ref[...] = jnp.zeros(tail_ref.shape, F32)

    xf = x_ref[0]
    h = _rmsnorm(xf, g_ref[...]).astype(BF16)
    for c in range(D_FF // FFN_CHUNK):
        lo = c * FFN_CHUNK
        hi = lo + FFN_CHUNK
        gate = jnp.dot(h, wup_ref[:, lo:hi], preferred_element_type=F32)
        up = jnp.dot(h, wup_ref[:, D_FF + lo:D_FF + hi], preferred_element_type=F32)
        ext = jnp.concatenate([tail_ref[:, lo:hi], gate], axis=0)
        tail_ref[:, lo:hi] = gate[tm - V7X_SUBLANES:, :]
        conv = (gate * wdw_ref[2:3, lo:hi]
                + ext[V7X_SUBLANES - 1:V7X_SUBLANES - 1 + tm] * wdw_ref[1:2, lo:hi]
                + ext[V7X_SUBLANES - 2:V7X_SUBLANES - 2 + tm] * wdw_ref[0:1, lo:hi]
                + bdw_ref[:, lo:hi])
        act_ref[:, lo:hi] = (conv * jax.nn.sigmoid(conv) * up).astype(BF16)
    out = xf + jnp.dot(act_ref[...], wdown_ref[...], preferred_element_type=F32)
    if final_norm:
        out = _rmsnorm(out, gfin_ref[...])
    o_ref[0] = out


def _resident(shape):
    zeros = (0,) * len(shape)
    return pl.BlockSpec(shape, lambda b, t: zeros, pipeline_mode=pl.Buffered(1))


def _token_spec(tile):
    return pl.BlockSpec((1, tile, D_MODEL), lambda b, t: (b, t, 0))


_COMPILER_PARAMS = pltpu.CompilerParams(
    dimension_semantics=("arbitrary", "arbitrary"),
    vmem_limit_bytes=V7X_VMEM_LIMIT_BYTES,
)


def _attn_layer(x, sinks, g, wqvt, bqvt, wk, bk, wo, bo):
    bsz, seq, _ = x.shape
    tile = TOKEN_TILE
    return pl.pallas_call(
        _attn_kernel,
        out_shape=jax.ShapeDtypeStruct(x.shape, F32),
        grid=(bsz, seq // tile),
        in_specs=[pl.BlockSpec(memory_space=pltpu.SMEM), _token_spec(tile)] + [
            _resident(a.shape) for a in (g, wqvt, bqvt, wk, bk, wo, bo)],
        out_specs=_token_spec(tile),
        scratch_shapes=[
            pltpu.VMEM((BLOCK + tile, KV_DIM), BF16),
            pltpu.VMEM((KV_DIM, BLOCK + tile), BF16),
            pltpu.VMEM((Q_DIM, tile), BF16),
            pltpu.VMEM((Q_DIM, tile), BF16),
        ],
        compiler_params=_COMPILER_PARAMS,
        name="swa_attention",
    )(sinks, x, g, wqvt, bqvt, wk, bk, wo, bo)


def _conv_layer(x, g, w1, b1, wdw, bdw, lng, lnb, w2, b2):
    bsz, seq, _ = x.shape
    tile = TOKEN_TILE
    n_lane = D_MODEL // V7X_LANES
    return pl.pallas_call(
        _conv_kernel,
        out_shape=jax.ShapeDtypeStruct(x.shape, F32),
        grid=(bsz, seq // tile),
        in_specs=[_token_spec(tile)] + [
            _resident(a.shape) for a in (g, w1, b1, wdw, bdw, lng, lnb, w2, b2)],
        out_specs=_token_spec(tile),
        scratch_shapes=[
            pltpu.VMEM((n_lane, CONV_PAD + tile, V7X_LANES), F32),
            pltpu.VMEM((n_lane, tile, V7X_LANES), F32),
        ],
        compiler_params=_COMPILER_PARAMS,
        name="conformer_conv",
    )(x, g, w1, b1, wdw, bdw, lng, lnb, w2, b2)


def _ffn_layer(x, g, wup, wdw, bdw, wdown, gfin, *, final_norm):
    bsz, seq, _ = x.shape
    tile = TOKEN_TILE
    return pl.pallas_call(
        functools.partial(_ffn_kernel, final_norm=final_norm),
        out_shape=jax.ShapeDtypeStruct(x.shape, F32),
        grid=(bsz, seq // tile),
        in_specs=[_token_spec(tile)] + [
            _resident(a.shape) for a in (g, wup, wdw, bdw, wdown, gfin)],
        out_specs=_token_spec(tile),
        scratch_shapes=[
            pltpu.VMEM((V7X_SUBLANES, D_FF), F32),
            pltpu.VMEM((tile, D_FF), BF16),
        ],
        compiler_params=_COMPILER_PARAMS,
        name="conv_ffn",
    )(x, g, wup, wdw, bdw, wdown, gfin)


def _row(v):
    return v.reshape(1, -1).astype(F32)


def kernel(x, norm_mix, attn_w_qkv, attn_b_qkv, attn_sinks, attn_w_o, attn_b_o,
           conv_w_pw1, conv_b_pw1, conv_w_dw, conv_b_dw, conv_ln_g, conv_ln_b,
           conv_w_pw2, conv_b_pw2, norm_ffn, ffn_w_up, ffn_w_dw, ffn_b_dw,
           ffn_w_down, final_norm):
    n_lane = D_MODEL // V7X_LANES
    gfin = _row(final_norm)

    wqkv = attn_w_qkv[0]
    bqkv = attn_b_qkv[0].astype(F32)
    wqvt = jnp.concatenate([wqkv[:, :Q_DIM], wqkv[:, Q_DIM + KV_DIM:]], axis=1).T.astype(BF16)
    bqvt = jnp.concatenate([bqkv[:Q_DIM], bqkv[Q_DIM + KV_DIM:]])
    bqvt = jnp.broadcast_to(bqvt[:, None], (Q_DIM + KV_DIM, V7X_LANES))
    x = _attn_layer(x, attn_sinks[0].astype(F32), _row(norm_mix[0]), wqvt, bqvt,
                    wqkv[:, Q_DIM:Q_DIM + KV_DIM].astype(BF16),
                    _row(bqkv[Q_DIM:Q_DIM + KV_DIM]), attn_w_o[0].astype(BF16),
                    _row(attn_b_o[0]))
    x = _ffn_layer(x, _row(norm_ffn[0]), ffn_w_up[0].astype(BF16),
                   ffn_w_dw[0].astype(F32), _row(ffn_b_dw[0]),
                   ffn_w_down[0].astype(BF16), gfin, final_norm=False)

    wdw = jnp.pad(conv_w_dw[0].astype(F32), ((0, CONV_PAD - CONV_WIDTH), (0, 0)))
    wdw = wdw.reshape(CONV_PAD, n_lane, V7X_LANES).transpose(1, 0, 2)
    bdw = conv_b_dw[0].astype(F32).reshape(n_lane, 1, V7X_LANES)
    x = _conv_layer(x, _row(norm_mix[1]), conv_w_pw1[0].astype(BF16),
                    _row(conv_b_pw1[0]), wdw, bdw, _row(conv_ln_g[0]),
                    _row(conv_ln_b[0]), conv_w_pw2[0].astype(BF16),
                    _row(conv_b_pw2[0]))
    x = _ffn_layer(x, _row(norm_ffn[1]), ffn_w_up[1].astype(BF16),
                   ffn_w_dw[1].astype(F32), _row(ffn_b_dw[1]),
                   ffn_w_down[1].astype(BF16), gfin, final_norm=True)
    return x
```

```python
import functools
import math

import jax
import jax.numpy as jnp
from jax import lax
from jax.experimental import pallas as pl
from jax.experimental.pallas import tpu as pltpu

D_MODEL = 1024
N_HEADS = 16
N_KV_HEADS = 2
GROUP = N_HEADS // N_KV_HEADS
HEAD_DIM = 64
WINDOW = 128
BLOCK = 128
CONV_WIDTH = 31
D_FF = 2816
FFN_CONV_WIDTH = 3
RMS_EPS = 1e-6
LN_EPS = 1e-5

Q_DIM = N_HEADS * HEAD_DIM
KV_DIM = N_KV_HEADS * HEAD_DIM
QKV_DIM = Q_DIM + 2 * KV_DIM

V7X_LANES = 128
V7X_SUBLANES = 8
V7X_MXU_DIM = 256
V7X_VMEM_LIMIT_BYTES = 56 * 1024 * 1024

TOKEN_TILE = 512
ATTN_LOOKAHEAD = 2
FFN_CHUNK = 256
CONV_CHUNK = 128
CONV_ROWS = 64
CONV_PAD = 32

F32 = jnp.float32
BF16 = jnp.bfloat16
NEG_MIN = float(jnp.finfo(jnp.float32).min)


def _rmsnorm(xf, g):
    ms = jnp.mean(xf * xf, axis=-1, keepdims=True)
    return xf * lax.rsqrt(ms + RMS_EPS) * g


def _pad_pitch(w):
    return jnp.pad(w, ((0, 0), (0, V7X_LANES)))


def _dot_padded(lhs, w_ref):
    return jnp.dot(lhs, w_ref[:, :w_ref.shape[1] - V7X_LANES], preferred_element_type=F32)


def _attn_kernel(sink_ref, x_ref, g_ref, wqvt_ref, bqvt_ref, wk_ref, bk_ref, wo_ref,
                 bo_ref, o_ref, k_ref, vt_ref, qt_ref, ot_ref):
    tq = x_ref.shape[1]
    n_blocks = tq // BLOCK
    t_idx = pl.program_id(1)
    nt_dims = (((1,), (1,)), ((), ()))
    tn_dims = (((0,), (0,)), ((), ()))

    @pl.when(t_idx == 0)
    def _():
        k_ref[0:BLOCK, :] = jnp.zeros((BLOCK, KV_DIM), BF16)
        vt_ref[:, 0:BLOCK] = jnp.zeros((KV_DIM, BLOCK), BF16)

    xf = x_ref[0]
    h = _rmsnorm(xf, g_ref[...]).astype(BF16)
    bias_t = jnp.concatenate([bqvt_ref[...]] * (tq // V7X_LANES), axis=1)
    qvt = lax.dot_general(wqvt_ref[...], h, nt_dims, preferred_element_type=F32) + bias_t
    qt_ref[...] = (qvt[:Q_DIM] * (1.0 / math.sqrt(HEAD_DIM))).astype(BF16)
    vt_ref[:, BLOCK:] = qvt[Q_DIM:].astype(BF16)
    kk = jnp.dot(h, wk_ref[...], preferred_element_type=F32) + bk_ref[...]
    k_ref[BLOCK:, :] = kk.astype(BF16)

    ki = lax.broadcasted_iota(jnp.int32, (2 * BLOCK, 2 * BLOCK), 0)
    qi = lax.broadcasted_iota(jnp.int32, (2 * BLOCK, 2 * BLOCK), 1) % BLOCK
    dist = qi + BLOCK - ki
    band = (dist >= 0) & (dist < WINDOW)
    second_head = lax.broadcasted_iota(jnp.int32, (1, 2 * BLOCK), 1) >= BLOCK
    zeros_qt = jnp.zeros((HEAD_DIM, BLOCK), BF16)
    ones_rows = jnp.ones((2 * V7X_SUBLANES, 2 * BLOCK), BF16)

    first_valid = band & ((ki >= BLOCK) | (t_idx > 0))

    def scores(j, head):
        c0 = j * BLOCK
        hkv = head // GROUP
        kb = k_ref[c0:c0 + 2 * BLOCK, :]
        cols = []
        for hd in (head, head + 1):
            qh = qt_ref[hd * HEAD_DIM:(hd + 1) * HEAD_DIM, c0:c0 + BLOCK]
            cols.append(jnp.concatenate(
                [qh, zeros_qt] if hkv == 0 else [zeros_qt, qh], axis=0))
        rhs = jnp.concatenate(cols, axis=1)
        return jnp.dot(kb, rhs, preferred_element_type=F32)

    def finish(j, head, st):
        c0 = j * BLOCK
        hkv = head // GROUP
        st = jnp.where(first_valid if j == 0 else band, st, NEG_MIN)
        sink = jnp.where(second_head, sink_ref[head + 1], sink_ref[head])
        m = jnp.maximum(jnp.max(st, axis=0, keepdims=True), sink)
        pt = jnp.exp(st - m).astype(BF16)
        vaug = jnp.concatenate(
            [vt_ref[hkv * HEAD_DIM:(hkv + 1) * HEAD_DIM, c0:c0 + 2 * BLOCK], ones_rows],
            axis=0)
        oa = jnp.dot(vaug, pt, preferred_element_type=F32)
        den = oa[HEAD_DIM:HEAD_DIM + 1, :] + jnp.exp(sink - m)
        ot = (oa[:HEAD_DIM, :] / den).astype(BF16)
        ot_ref[head * HEAD_DIM:(head + 1) * HEAD_DIM, c0:c0 + BLOCK] = ot[:, :BLOCK]
        ot_ref[(head + 1) * HEAD_DIM:(head + 2) * HEAD_DIM, c0:c0 + BLOCK] = ot[:, BLOCK:]

    items = [(j, head) for j in range(n_blocks) for head in range(0, N_HEADS, 2)]
    pending = [scores(*it) for it in items[:ATTN_LOOKAHEAD]]
    for n, it in enumerate(items):
        if n + ATTN_LOOKAHEAD < len(items):
            pending.append(scores(*items[n + ATTN_LOOKAHEAD]))
        finish(*it, pending.pop(0))

    k_ref[0:BLOCK, :] = k_ref[tq:tq + BLOCK, :]
    vt_ref[:, 0:BLOCK] = vt_ref[:, tq:tq + BLOCK]
    out = lax.dot_general(ot_ref[...], wo_ref[...], tn_dims, preferred_element_type=F32)
    o_ref[0] = xf + out + bo_ref[...]


def _conv_kernel(x_ref, g_ref, w1_ref, b1_ref, wdw_ref, bdw_ref, lng_ref, lnb_ref,
                 w2_ref, b2_ref, o_ref, u_ref):
    tc = x_ref.shape[1]
    n_lane = D_MODEL // V7X_LANES
    n_chunks = tc // CONV_CHUNK
    first_tap_row = CONV_PAD - (CONV_WIDTH - 1)
    t_idx = pl.program_id(1)

    @pl.when(t_idx == 0)
    def _():
        u_ref[:, 0:CONV_PAD, :] = jnp.zeros((n_lane, CONV_PAD, V7X_LANES), F32)

    def glu_stage(r):
        lo = r * CONV_CHUNK
        h = _rmsnorm(x_ref[0, lo:lo + CONV_CHUNK, :], g_ref[...]).astype(BF16)
        a = _dot_padded(h, w1_ref) + b1_ref[...]
        u = a[:, :D_MODEL] * jax.nn.sigmoid(a[:, D_MODEL:])
        for l in range(n_lane):
            u_ref[l, CONV_PAD + lo:CONV_PAD + lo + CONV_CHUNK, :] = (
                u[:, l * V7X_LANES:(l + 1) * V7X_LANES])

    def conv_stage(r):
        lanes = []
        for l in range(n_lane):
            rows = []
            for s in range(CONV_CHUNK // CONV_ROWS):
                r0 = r * CONV_CHUNK + s * CONV_ROWS + first_tap_row
                acc = jnp.broadcast_to(bdw_ref[l], (CONV_ROWS, V7X_LANES))
                for j in range(CONV_WIDTH):
                    acc = acc + u_ref[l, r0 + j:r0 + j + CONV_ROWS, :] * wdw_ref[l, j:j + 1, :]
                rows.append(acc)
            lanes.append(jnp.concatenate(rows, axis=0))
        y = jnp.concatenate(lanes, axis=1)
        mu = jnp.mean(y, axis=-1, keepdims=True)
        yc = y - mu
        var = jnp.mean(yc * yc, axis=-1, keepdims=True)
        z = yc * lax.rsqrt(var + LN_EPS) * lng_ref[...] + lnb_ref[...]
        return (z * jax.nn.sigmoid(z)).astype(BF16)

    def out_stage(r, z):
        lo = r * CONV_CHUNK
        out = _dot_padded(z, w2_ref)
        o_ref[0, lo:lo + CONV_CHUNK, :] = x_ref[0, lo:lo + CONV_CHUNK, :] + out + b2_ref[...]

    glu_stage(0)
    for r in range(n_chunks):
        if r + 1 < n_chunks:
            glu_stage(r + 1)
        out_stage(r, conv_stage(r))

    for l in range(n_lane):
        u_ref[l, 0:CONV_PAD, :] = u_ref[l, tc:tc + CONV_PAD, :]


def _ffn_kernel(x_ref, g_ref, wup_ref, wdw_ref, bdw_ref, wdown_ref, gfin_ref,
                o_ref, tail_ref, act_ref, *, final_norm):
    tm = x_ref.shape[1]
    t_idx = pl.program_id(1)

    @pl.when(t_idx == 0)
    def _():
        tail_ref[...] = jnp.zeros(tail_ref.shape, F32)

    xf = x_ref[0]
    h = _rmsnorm(xf, g_ref[...]).astype(BF16)
    for c in range(D_FF // FFN_CHUNK):
        lo = c * FFN_CHUNK
        hi = lo + FFN_CHUNK
        gate = jnp.dot(h, wup_ref[:, lo:hi], preferred_element_type=F32)
        up = jnp.dot(h, wup_ref[:, D_FF + lo:D_FF + hi], preferred_element_type=F32)
        ext = jnp.concatenate([tail_ref[:, lo:hi], gate], axis=0)
        tail_ref[:, lo:hi] = gate[tm - V7X_SUBLANES:, :]
        conv = (gate * wdw_ref[2:3, lo:hi]
                + ext[V7X_SUBLANES - 1:V7X_SUBLANES - 1 + tm] * wdw_ref[1:2, lo:hi]
                + ext[V7X_SUBLANES - 2:V7X_SUBLANES - 2 + tm] * wdw_ref[0:1, lo:hi]
                + bdw_ref[:, lo:hi])
        act_ref[:, lo:hi] = (conv * jax.nn.sigmoid(conv) * up).astype(BF16)
    out = xf + _dot_padded(act_ref[...], wdown_ref)
    if final_norm:
        out = _rmsnorm(out, gfin_ref[...])
    o_ref[0] = out


def _resident(shape):
    zeros = (0,) * len(shape)
    return pl.BlockSpec(shape, lambda b, t: zeros, pipeline_mode=pl.Buffered(1))


def _token_spec(tile):
    return pl.BlockSpec((1, tile, D_MODEL), lambda b, t: (b, t, 0))


_COMPILER_PARAMS = pltpu.CompilerParams(
    dimension_semantics=("arbitrary", "arbitrary"),
    vmem_limit_bytes=V7X_VMEM_LIMIT_BYTES,
)


def _attn_layer(x, sinks, g, wqvt, bqvt, wk, bk, wo, bo):
    bsz, seq, _ = x.shape
    tile = TOKEN_TILE
    return pl.pallas_call(
        _attn_kernel,
        out_shape=jax.ShapeDtypeStruct(x.shape, F32),
        grid=(bsz, seq // tile),
        in_specs=[pl.BlockSpec(memory_space=pltpu.SMEM), _token_spec(tile)] + [
            _resident(a.shape) for a in (g, wqvt, bqvt, wk, bk, wo, bo)],
        out_specs=_token_spec(tile),
        scratch_shapes=[
            pltpu.VMEM((BLOCK + tile, KV_DIM), BF16),
            pltpu.VMEM((KV_DIM, BLOCK + tile), BF16),
            pltpu.VMEM((Q_DIM, tile), BF16),
            pltpu.VMEM((Q_DIM, tile), BF16),
        ],
        compiler_params=_COMPILER_PARAMS,
        name="swa_attention",
    )(sinks, x, g, wqvt, bqvt, wk, bk, wo, bo)


def _conv_layer(x, g, w1, b1, wdw, bdw, lng, lnb, w2, b2):
    bsz, seq, _ = x.shape
    tile = TOKEN_TILE
    n_lane = D_MODEL // V7X_LANES
    return pl.pallas_call(
        _conv_kernel,
        out_shape=jax.ShapeDtypeStruct(x.shape, F32),
        grid=(bsz, seq // tile),
        in_specs=[_token_spec(tile)] + [
            _resident(a.shape) for a in (g, w1, b1, wdw, bdw, lng, lnb, w2, b2)],
        out_specs=_token_spec(tile),
        scratch_shapes=[pltpu.VMEM((n_lane, CONV_PAD + tile, V7X_LANES), F32)],
        compiler_params=_COMPILER_PARAMS,
        name="conformer_conv",
    )(x, g, w1, b1, wdw, bdw, lng, lnb, w2, b2)


def _ffn_layer(x, g, wup, wdw, bdw, wdown, gfin, *, final_norm):
    bsz, seq, _ = x.shape
    tile = TOKEN_TILE
    return pl.pallas_call(
        functools.partial(_ffn_kernel, final_norm=final_norm),
        out_shape=jax.ShapeDtypeStruct(x.shape, F32),
        grid=(bsz, seq // tile),
        in_specs=[_token_spec(tile)] + [
            _resident(a.shape) for a in (g, wup, wdw, bdw, wdown, gfin)],
        out_specs=_token_spec(tile),
        scratch_shapes=[
            pltpu.VMEM((V7X_SUBLANES, D_FF), F32),
            pltpu.VMEM((tile, D_FF), BF16),
        ],
        compiler_params=_COMPILER_PARAMS,
        name="conv_ffn",
    )(x, g, wup, wdw, bdw, wdown, gfin)


def _row(v):
    return v.reshape(1, -1).astype(F32)


def kernel(x, norm_mix, attn_w_qkv, attn_b_qkv, attn_sinks, attn_w_o, attn_b_o,
           conv_w_pw1, conv_b_pw1, conv_w_dw, conv_b_dw, conv_ln_g, conv_ln_b,
           conv_w_pw2, conv_b_pw2, norm_ffn, ffn_w_up, ffn_w_dw, ffn_b_dw,
           ffn_w_down, final_norm):
    n_lane = D_MODEL // V7X_LANES
    gfin = _row(final_norm)

    wqkv = attn_w_qkv[0]
    bqkv = attn_b_qkv[0].astype(F32)
    wqvt = jnp.concatenate([wqkv[:, :Q_DIM], wqkv[:, Q_DIM + KV_DIM:]], axis=1).T.astype(BF16)
    bqvt = jnp.concatenate([bqkv[:Q_DIM], bqkv[Q_DIM + KV_DIM:]])
    bqvt = jnp.broadcast_to(bqvt[:, None], (Q_DIM + KV_DIM, V7X_LANES))
    x = _attn_layer(x, attn_sinks[0].astype(F32), _row(norm_mix[0]), wqvt, bqvt,
                    wqkv[:, Q_DIM:Q_DIM + KV_DIM].astype(BF16),
                    _row(bqkv[Q_DIM:Q_DIM + KV_DIM]), attn_w_o[0].astype(BF16),
                    _row(attn_b_o[0]))
    x = _ffn_layer(x, _row(norm_ffn[0]), ffn_w_up[0].astype(BF16),
                   ffn_w_dw[0].astype(F32), _row(ffn_b_dw[0]),
                   _pad_pitch(ffn_w_down[0].astype(BF16)), gfin, final_norm=False)

    wdw = jnp.pad(conv_w_dw[0].astype(F32), ((0, CONV_PAD - CONV_WIDTH), (0, 0)))
    wdw = wdw.reshape(CONV_PAD, n_lane, V7X_LANES).transpose(1, 0, 2)
    bdw = conv_b_dw[0].astype(F32).reshape(n_lane, 1, V7X_LANES)
    x = _conv_layer(x, _row(norm_mix[1]), _pad_pitch(conv_w_pw1[0].astype(BF16)),
                    _row(conv_b_pw1[0]), wdw, bdw, _row(conv_ln_g[0]),
                    _row(conv_ln_b[0]), _pad_pitch(conv_w_pw2[0].astype(BF16)),
                    _row(conv_b_pw2[0]))
    x = _ffn_layer(x, _row(norm_ffn[1]), ffn_w_up[1].astype(BF16),
                   ffn_w_dw[1].astype(F32), _row(ffn_b_dw[1]),
                   _pad_pitch(ffn_w_down[1].astype(BF16)), gfin, final_norm=True)
    return x
```

```python
import functools
import math

import jax
import jax.numpy as jnp
from jax import lax
from jax.experimental import pallas as pl
from jax.experimental.pallas import tpu as pltpu

D_MODEL = 1024
N_HEADS = 16
N_KV_HEADS = 2
GROUP = N_HEADS // N_KV_HEADS
HEAD_DIM = 64
WINDOW = 128
BLOCK = 128
CONV_WIDTH = 31
D_FF = 2816
FFN_CONV_WIDTH = 3
RMS_EPS = 1e-6
LN_EPS = 1e-5

Q_DIM = N_HEADS * HEAD_DIM
KV_DIM = N_KV_HEADS * HEAD_DIM

V7X_LANES = 128
V7X_SUBLANES = 8
V7X_MXU_DIM = 256
V7X_VMEM_LIMIT_BYTES = 56 * 1024 * 1024

TOKEN_TILE = 512
ATTN_LOOKAHEAD = 2
FFN_CHUNK = V7X_MXU_DIM
CONV_CHUNK = 128
CONV_ROWS = 64
CONV_PAD = 32
N_LANE_CHUNKS = D_MODEL // V7X_LANES

F32 = jnp.float32
BF16 = jnp.bfloat16
NEG_MIN = float(jnp.finfo(jnp.float32).min)


def _rmsnorm(xf, g):
    ms = jnp.mean(xf * xf, axis=-1, keepdims=True)
    return xf * lax.rsqrt(ms + RMS_EPS) * g


def _pad_pitch(w):
    return jnp.pad(w, ((0, 0), (0, V7X_LANES)))


def _dot_padded(lhs, w_ref):
    return jnp.dot(lhs, w_ref[:, :w_ref.shape[1] - V7X_LANES], preferred_element_type=F32)


def _interleave(primary, secondary):
    merged = []
    done = 0
    for i, item in enumerate(primary):
        merged.append(item)
        target = ((i + 1) * len(secondary)) // len(primary)
        merged.extend(secondary[done:target])
        done = target
    return merged


def _ffn_items(xin_ref, g_ref, wup_ref, wdw_ref, bdw_ref, wdown_ref, tail_ref, act_ref,
               out_ref):
    tm = xin_ref.shape[0]
    state = {}

    def begin():
        state["h"] = _rmsnorm(xin_ref[...], g_ref[...]).astype(BF16)

    def up_item(lo):
        def run():
            hi = lo + FFN_CHUNK
            h = state["h"]
            gate = jnp.dot(h, wup_ref[:, lo:hi], preferred_element_type=F32)
            up = jnp.dot(h, wup_ref[:, D_FF + lo:D_FF + hi], preferred_element_type=F32)
            ext = jnp.concatenate([tail_ref[:, lo:hi], gate], axis=0)
            tail_ref[:, lo:hi] = gate[tm - V7X_SUBLANES:, :]
            conv = (gate * wdw_ref[2:3, lo:hi]
                    + ext[V7X_SUBLANES - 1:V7X_SUBLANES - 1 + tm] * wdw_ref[1:2, lo:hi]
                    + ext[V7X_SUBLANES - 2:V7X_SUBLANES - 2 + tm] * wdw_ref[0:1, lo:hi]
                    + bdw_ref[:, lo:hi])
            act_ref[:, lo:hi] = (conv * jax.nn.sigmoid(conv) * up).astype(BF16)
        return run

    def down_item(lo):
        def run():
            hi = lo + FFN_CHUNK
            out_ref[0, :, lo:hi] = xin_ref[:, lo:hi] + jnp.dot(
                act_ref[...], wdown_ref[:, lo:hi], preferred_element_type=F32)
        return run

    items = ([up_item(lo) for lo in range(0, D_FF, FFN_CHUNK)]
             + [down_item(lo) for lo in range(0, D_MODEL, FFN_CHUNK)])
    return begin, items


def _layer0_kernel(sink_ref, x_ref, ga_ref, wqvt_ref, bqvt_ref, wk_ref, bk_ref, wo_ref,
                   bo_ref, gf_ref, wup_ref, wdw_ref, bdw_ref, wdown_ref,
                   o_ref, k_ref, vt_ref, qt_ref, ot_ref, mid_ref, tail_ref, act_ref):
    tq = x_ref.shape[1]
    n_blocks = tq // BLOCK
    t_idx = pl.program_id(1)
    nt_dims = (((1,), (1,)), ((), ()))
    tn_dims = (((0,), (0,)), ((), ()))

    @pl.when(t_idx == 0)
    def _():
        k_ref[0:BLOCK, :] = jnp.zeros((BLOCK, KV_DIM), BF16)
        vt_ref[:, 0:BLOCK] = jnp.zeros((KV_DIM, BLOCK), BF16)
        mid_ref[...] = jnp.zeros(mid_ref.shape, F32)
        tail_ref[...] = jnp.zeros(tail_ref.shape, F32)

    h = _rmsnorm(x_ref[0], ga_ref[...]).astype(BF16)
    bias_t = jnp.concatenate([bqvt_ref[...]] * (tq // V7X_LANES), axis=1)
    qvt = lax.dot_general(wqvt_ref[...], h, nt_dims, preferred_element_type=F32) + bias_t
    qt_ref[...] = (qvt[:Q_DIM] * (1.0 / math.sqrt(HEAD_DIM))).astype(BF16)
    vt_ref[:, BLOCK:] = qvt[Q_DIM:].astype(BF16)
    kk = jnp.dot(h, wk_ref[...], preferred_element_type=F32) + bk_ref[...]
    k_ref[BLOCK:, :] = kk.astype(BF16)

    ffn_begin, ffn_items = _ffn_items(mid_ref, gf_ref, wup_ref, wdw_ref, bdw_ref,
                                      wdown_ref, tail_ref, act_ref, o_ref)
    ffn_begin()

    ki = lax.broadcasted_iota(jnp.int32, (2 * BLOCK, 2 * BLOCK), 0)
    qi = lax.broadcasted_iota(jnp.int32, (2 * BLOCK, 2 * BLOCK), 1) % BLOCK
    dist = qi + BLOCK - ki
    band = (dist >= 0) & (dist < WINDOW)
    first_valid = band & ((ki >= BLOCK) | (t_idx > 0))
    second_head = lax.broadcasted_iota(jnp.int32, (1, 2 * BLOCK), 1) >= BLOCK
    zeros_qt = jnp.zeros((HEAD_DIM, BLOCK), BF16)
    ones_rows = jnp.ones((2 * V7X_SUBLANES, 2 * BLOCK), BF16)

    def scores(j, head):
        c0 = j * BLOCK
        hkv = head // GROUP
        kb = k_ref[c0:c0 + 2 * BLOCK, :]
        cols = []
        for hd in (head, head + 1):
            qh = qt_ref[hd * HEAD_DIM:(hd + 1) * HEAD_DIM, c0:c0 + BLOCK]
            cols.append(jnp.concatenate(
                [qh, zeros_qt] if hkv == 0 else [zeros_qt, qh], axis=0))
        rhs = jnp.concatenate(cols, axis=1)
        return jnp.dot(kb, rhs, preferred_element_type=F32)

    def finish(j, head, st):
        c0 = j * BLOCK
        hkv = head // GROUP
        st = jnp.where(first_valid if j == 0 else band, st, NEG_MIN)
        sink = jnp.where(second_head, sink_ref[head + 1], sink_ref[head])
        m = jnp.maximum(jnp.max(st, axis=0, keepdims=True), sink)
        pt = jnp.exp(st - m).astype(BF16)
        vaug = jnp.concatenate(
            [vt_ref[hkv * HEAD_DIM:(hkv + 1) * HEAD_DIM, c0:c0 + 2 * BLOCK], ones_rows],
            axis=0)
        oa = jnp.dot(vaug, pt, preferred_element_type=F32)
        den = oa[HEAD_DIM:HEAD_DIM + 1, :] + jnp.exp(sink - m)
        ot = (oa[:HEAD_DIM, :] / den).astype(BF16)
        ot_ref[head * HEAD_DIM:(head + 1) * HEAD_DIM, c0:c0 + BLOCK] = ot[:, :BLOCK]
        ot_ref[(head + 1) * HEAD_DIM:(head + 2) * HEAD_DIM, c0:c0 + BLOCK] = ot[:, BLOCK:]

    work = [(j, head) for j in range(n_blocks) for head in range(0, N_HEADS, 2)]
    pending = [scores(*w) for w in work[:ATTN_LOOKAHEAD]]

    def attn_item(n):
        def run():
            if n + ATTN_LOOKAHEAD < len(work):
                pending.append(scores(*work[n + ATTN_LOOKAHEAD]))
            finish(*work[n], pending.pop(0))
        return run

    for item in _interleave(ffn_items, [attn_item(n) for n in range(len(work))]):
        item()

    k_ref[0:BLOCK, :] = k_ref[tq:tq + BLOCK, :]
    vt_ref[:, 0:BLOCK] = vt_ref[:, tq:tq + BLOCK]
    out = lax.dot_general(ot_ref[...], wo_ref[...], tn_dims, preferred_element_type=F32)
    mid_ref[...] = x_ref[0] + out + bo_ref[...]


def _conv_kernel(x_ref, g_ref, w1_ref, b1_ref, wdw_ref, bdw_ref, lng_ref, lnb_ref,
                 w2_ref, b2_ref, o_ref, u_ref):
    tc = x_ref.shape[1]
    n_chunks = tc // CONV_CHUNK
    first_tap_row = CONV_PAD - (CONV_WIDTH - 1)
    t_idx = pl.program_id(1)

    @pl.when(t_idx == 0)
    def _():
        u_ref[:, 0:CONV_PAD, :] = jnp.zeros((N_LANE_CHUNKS, CONV_PAD, V7X_LANES), F32)

    def glu_stage(r):
        lo = r * CONV_CHUNK
        h = _rmsnorm(x_ref[0, lo:lo + CONV_CHUNK, :], g_ref[...]).astype(BF16)
        a = _dot_padded(h, w1_ref) + b1_ref[...]
        u = a[:, :D_MODEL] * jax.nn.sigmoid(a[:, D_MODEL:])
        for l in range(N_LANE_CHUNKS):
            u_ref[l, CONV_PAD + lo:CONV_PAD + lo + CONV_CHUNK, :] = (
                u[:, l * V7X_LANES:(l + 1) * V7X_LANES])

    def conv_stage(r):
        lanes = []
        for l in range(N_LANE_CHUNKS):
            rows = []
            for s in range(CONV_CHUNK // CONV_ROWS):
                r0 = r * CONV_CHUNK + s * CONV_ROWS + first_tap_row
                acc = jnp.broadcast_to(bdw_ref[l], (CONV_ROWS, V7X_LANES))
                for j in range(CONV_WIDTH):
                    acc = acc + u_ref[l, r0 + j:r0 + j + CONV_ROWS, :] * wdw_ref[l, j:j + 1, :]
                rows.append(acc)
            lanes.append(jnp.concatenate(rows, axis=0))
        y = jnp.concatenate(lanes, axis=1)
        mu = jnp.mean(y, axis=-1, keepdims=True)
        yc = y - mu
        var = jnp.mean(yc * yc, axis=-1, keepdims=True)
        z = yc * lax.rsqrt(var + LN_EPS) * lng_ref[...] + lnb_ref[...]
        return (z * jax.nn.sigmoid(z)).astype(BF16)

    def out_stage(r, z):
        lo = r * CONV_CHUNK
        out = _dot_padded(z, w2_ref)
        o_ref[0, lo:lo + CONV_CHUNK, :] = x_ref[0, lo:lo + CONV_CHUNK, :] + out + b2_ref[...]

    glu_stage(0)
    for r in range(n_chunks):
        if r + 1 < n_chunks:
            glu_stage(r + 1)
        out_stage(r, conv_stage(r))

    for l in range(N_LANE_CHUNKS):
        u_ref[l, 0:CONV_PAD, :] = u_ref[l, tc:tc + CONV_PAD, :]


def _ffn_final_kernel(x_ref, g_ref, wup_ref, wdw_ref, bdw_ref, wdown_ref, gfin_ref,
                      o_ref, tail_ref, act_ref):
    @pl.when(pl.program_id(1) == 0)
    def _():
        tail_ref[...] = jnp.zeros(tail_ref.shape, F32)

    begin, items = _ffn_items(x_ref.at[0], g_ref, wup_ref, wdw_ref, bdw_ref, wdown_ref,
                              tail_ref, act_ref, o_ref)
    begin()
    for item in items:
        item()
    o_ref[0] = _rmsnorm(o_ref[0], gfin_ref[...])


def _resident(shape):
    zeros = (0,) * len(shape)
    return pl.BlockSpec(shape, lambda b, t: zeros, pipeline_mode=pl.Buffered(1))


def _token_spec():
    return pl.BlockSpec((1, TOKEN_TILE, D_MODEL), lambda b, t: (b, t, 0))


def _skewed_specs(n_tiles):
    block = (1, TOKEN_TILE, D_MODEL)
    in_spec = pl.BlockSpec(block, lambda b, t: (b, jnp.minimum(t, n_tiles - 1), 0))
    out_spec = pl.BlockSpec(block, lambda b, t: (b, jnp.maximum(t - 1, 0), 0))
    return in_spec, out_spec


_COMPILER_PARAMS = pltpu.CompilerParams(
    dimension_semantics=("arbitrary", "arbitrary"),
    vmem_limit_bytes=V7X_VMEM_LIMIT_BYTES,
)


def _layer0(x, sinks, attn_params, ffn_params):
    bsz, seq, _ = x.shape
    n_tiles = seq // TOKEN_TILE
    in_spec, out_spec = _skewed_specs(n_tiles)
    params = tuple(attn_params) + tuple(ffn_params)
    return pl.pallas_call(
        _layer0_kernel,
        out_shape=jax.ShapeDtypeStruct(x.shape, F32),
        grid=(bsz, n_tiles + 1),
        in_specs=[pl.BlockSpec(memory_space=pltpu.SMEM), in_spec] + [
            _resident(a.shape) for a in params],
        out_specs=out_spec,
        scratch_shapes=[
            pltpu.VMEM((BLOCK + TOKEN_TILE, KV_DIM), BF16),
            pltpu.VMEM((KV_DIM, BLOCK + TOKEN_TILE), BF16),
            pltpu.VMEM((Q_DIM, TOKEN_TILE), BF16),
            pltpu.VMEM((Q_DIM, TOKEN_TILE), BF16),
            pltpu.VMEM((TOKEN_TILE, D_MODEL), F32),
            pltpu.VMEM((V7X_SUBLANES, D_FF), F32),
            pltpu.VMEM((TOKEN_TILE, D_FF), BF16),
        ],
        compiler_params=_COMPILER_PARAMS,
        name="layer0_swa_ffn",
    )(sinks, x, *params)


def _conv_layer(x, conv_params):
    bsz, seq, _ = x.shape
    return pl.pallas_call(
        _conv_kernel,
        out_shape=jax.ShapeDtypeStruct(x.shape, F32),
        grid=(bsz, seq // TOKEN_TILE),
        in_specs=[_token_spec()] + [_resident(a.shape) for a in conv_params],
        out_specs=_token_spec(),
        scratch_shapes=[
            pltpu.VMEM((N_LANE_CHUNKS, CONV_PAD + TOKEN_TILE, V7X_LANES), F32)],
        compiler_params=_COMPILER_PARAMS,
        name="conformer_conv",
    )(x, *conv_params)


def _ffn_final_layer(x, ffn_params, gfin):
    bsz, seq, _ = x.shape
    params = tuple(ffn_params) + (gfin,)
    return pl.pallas_call(
        _ffn_final_kernel,
        out_shape=jax.ShapeDtypeStruct(x.shape, F32),
        grid=(bsz, seq // TOKEN_TILE),
        in_specs=[_token_spec()] + [_resident(a.shape) for a in params],
        out_specs=_token_spec(),
        scratch_shapes=[
            pltpu.VMEM((V7X_SUBLANES, D_FF), F32),
            pltpu.VMEM((TOKEN_TILE, D_FF), BF16),
        ],
        compiler_params=_COMPILER_PARAMS,
        name="conv_ffn_final",
    )(x, *params)


def _row(v):
    return v.reshape(1, -1).astype(F32)


def _ffn_params(i, norm_ffn, ffn_w_up, ffn_w_dw, ffn_b_dw, ffn_w_down):
    return (_row(norm_ffn[i]), ffn_w_up[i].astype(BF16), ffn_w_dw[i].astype(F32),
            _row(ffn_b_dw[i]), _pad_pitch(ffn_w_down[i].astype(BF16)))


def kernel(x, norm_mix, attn_w_qkv, attn_b_qkv, attn_sinks, attn_w_o, attn_b_o,
           conv_w_pw1, conv_b_pw1, conv_w_dw, conv_b_dw, conv_ln_g, conv_ln_b,
           conv_w_pw2, conv_b_pw2, norm_ffn, ffn_w_up, ffn_w_dw, ffn_b_dw,
           ffn_w_down, final_norm):
    ffn = functools.partial(_ffn_params, norm_ffn=norm_ffn, ffn_w_up=ffn_w_up,
                            ffn_w_dw=ffn_w_dw, ffn_b_dw=ffn_b_dw, ffn_w_down=ffn_w_down)

    wqkv = attn_w_qkv[0]
    bqkv = attn_b_qkv[0].astype(F32)
    wqvt = jnp.concatenate([wqkv[:, :Q_DIM], wqkv[:, Q_DIM + KV_DIM:]], axis=1).T.astype(BF16)
    bqvt = jnp.concatenate([bqkv[:Q_DIM], bqkv[Q_DIM + KV_DIM:]])
    bqvt = jnp.broadcast_to(bqvt[:, None], (Q_DIM + KV_DIM, V7X_LANES))
    attn_params = (_row(norm_mix[0]), wqvt, bqvt,
                   wqkv[:, Q_DIM:Q_DIM + KV_DIM].astype(BF16),
                   _row(bqkv[Q_DIM:Q_DIM + KV_DIM]), attn_w_o[0].astype(BF16),
                   _row(attn_b_o[0]))
    x = _layer0(x, attn_sinks[0].astype(F32), attn_params, ffn(0))

    wdw = jnp.pad(conv_w_dw[0].astype(F32), ((0, CONV_PAD - CONV_WIDTH), (0, 0)))
    wdw = wdw.reshape(CONV_PAD, N_LANE_CHUNKS, V7X_LANES).transpose(1, 0, 2)
    bdw = conv_b_dw[0].astype(F32).reshape(N_LANE_CHUNKS, 1, V7X_LANES)
    conv_params = (_row(norm_mix[1]), _pad_pitch(conv_w_pw1[0].astype(BF16)),
                   _row(conv_b_pw1[0]), wdw, bdw, _row(conv_ln_g[0]), _row(conv_ln_b[0]),
                   _pad_pitch(conv_w_pw2[0].astype(BF16)), _row(conv_b_pw2[0]))
    x = _conv_layer(x, conv_params)
    return _ffn_final_layer(x, ffn(1), _row(final_norm))
```

```python
import functools
import math

import jax
import jax.numpy as jnp
from jax import lax
from jax.experimental import pallas as pl
from jax.experimental.pallas import tpu as pltpu

D_MODEL = 1024
N_HEADS = 16
N_KV_HEADS = 2
GROUP = N_HEADS // N_KV_HEADS
HEAD_DIM = 64
WINDOW = 128
BLOCK = 128
CONV_WIDTH = 31
D_FF = 2816
FFN_CONV_WIDTH = 3
RMS_EPS = 1e-6
LN_EPS = 1e-5

Q_DIM = N_HEADS * HEAD_DIM
KV_DIM = N_KV_HEADS * HEAD_DIM

V7X_LANES = 128
V7X_SUBLANES = 8
V7X_MXU_DIM = 256
V7X_VMEM_LIMIT_BYTES = 56 * 1024 * 1024

TOKEN_TILE = 512
FFN_FINAL_TILE = 1024
ATTN_LOOKAHEAD = 3
FFN_CHUNK = V7X_MXU_DIM
CONV_CHUNK = 128
CONV_ROWS = 64
CONV_PAD = 32
N_LANE_CHUNKS = D_MODEL // V7X_LANES

F32 = jnp.float32
BF16 = jnp.bfloat16
NEG_MIN = float(jnp.finfo(jnp.float32).min)


def _rmsnorm(xf, g):
    ms = jnp.mean(xf * xf, axis=-1, keepdims=True)
    return xf * lax.rsqrt(ms + RMS_EPS) * g


def _pad_pitch(w):
    return jnp.pad(w, ((0, 0),) * (w.ndim - 1) + ((0, V7X_LANES),))


def _dot_padded(lhs, w_ref):
    return jnp.dot(lhs, w_ref[:, :w_ref.shape[1] - V7X_LANES], preferred_element_type=F32)


def _interleave(primary, secondary):
    merged = []
    done = 0
    for i, item in enumerate(primary):
        merged.append(item)
        target = ((i + 1) * len(secondary)) // len(primary)
        merged.extend(secondary[done:target])
        done = target
    return merged


def _ffn_items(xin_ref, g_ref, wup_ref, wdw_ref, bdw_ref, wdown_ref, tail_ref, act_ref,
               out_ref):
    tm = xin_ref.shape[0]
    state = {}

    def begin():
        state["h"] = _rmsnorm(xin_ref[...], g_ref[...]).astype(BF16)

    def up_item(lo):
        def run():
            hi = lo + FFN_CHUNK
            h = state["h"]
            gate = jnp.dot(h, wup_ref[:, lo:hi], preferred_element_type=F32)
            up = jnp.dot(h, wup_ref[:, D_FF + lo:D_FF + hi], preferred_element_type=F32)
            ext = jnp.concatenate([tail_ref[:, lo:hi], gate], axis=0)
            tail_ref[:, lo:hi] = gate[tm - V7X_SUBLANES:, :]
            conv = (gate * wdw_ref[2:3, lo:hi]
                    + ext[V7X_SUBLANES - 1:V7X_SUBLANES - 1 + tm] * wdw_ref[1:2, lo:hi]
                    + ext[V7X_SUBLANES - 2:V7X_SUBLANES - 2 + tm] * wdw_ref[0:1, lo:hi]
                    + bdw_ref[:, lo:hi])
            act_ref[:, lo:hi] = (conv * jax.nn.sigmoid(conv) * up).astype(BF16)
        return run

    def down_item(lo):
        def run():
            hi = lo + FFN_CHUNK
            out_ref[0, :, lo:hi] = xin_ref[:, lo:hi] + jnp.dot(
                act_ref[...], wdown_ref[:, lo:hi], preferred_element_type=F32)
        return run

    items = ([up_item(lo) for lo in range(0, D_FF, FFN_CHUNK)]
             + [down_item(lo) for lo in range(0, D_MODEL, FFN_CHUNK)])
    return begin, items


def _layer0_kernel(sink_ref, x_ref, ga_ref, wqvt_ref, bqvt_ref, wk_ref, bk_ref, wo_ref,
                   bo_ref, gf_ref, wup_ref, wdw_ref, bdw_ref, wdown_ref,
                   o_ref, k_ref, vt_ref, qt_ref, ot_ref, mid_ref, tail_ref, act_ref):
    tq = x_ref.shape[1]
    n_blocks = tq // BLOCK
    t_idx = pl.program_id(1)
    nt_dims = (((1,), (1,)), ((), ()))
    tn_dims = (((0,), (0,)), ((), ()))

    @pl.when(t_idx == 0)
    def _():
        k_ref[0:BLOCK, :] = jnp.zeros((BLOCK, KV_DIM), BF16)
        vt_ref[:, 0:BLOCK] = jnp.zeros((KV_DIM, BLOCK), BF16)
        mid_ref[...] = jnp.zeros(mid_ref.shape, F32)
        tail_ref[...] = jnp.zeros(tail_ref.shape, F32)

    h = _rmsnorm(x_ref[0], ga_ref[...]).astype(BF16)
    bias_t = jnp.concatenate([bqvt_ref[...]] * (tq // V7X_LANES), axis=1)
    qvt = lax.dot_general(wqvt_ref[...], h, nt_dims, preferred_element_type=F32) + bias_t
    qt_ref[...] = (qvt[:Q_DIM] * (1.0 / math.sqrt(HEAD_DIM))).astype(BF16)
    vt_ref[:, BLOCK:] = qvt[Q_DIM:].astype(BF16)
    kk = jnp.dot(h, wk_ref[...], preferred_element_type=F32) + bk_ref[...]
    k_ref[BLOCK:, :] = kk.astype(BF16)

    ffn_begin, ffn_items = _ffn_items(mid_ref, gf_ref, wup_ref, wdw_ref, bdw_ref,
                                      wdown_ref, tail_ref, act_ref, o_ref)
    ffn_begin()

    ki = lax.broadcasted_iota(jnp.int32, (2 * BLOCK, 2 * BLOCK), 0)
    qi = lax.broadcasted_iota(jnp.int32, (2 * BLOCK, 2 * BLOCK), 1) % BLOCK
    dist = qi + BLOCK - ki
    band = (dist >= 0) & (dist < WINDOW)
    first_valid = band & ((ki >= BLOCK) | (t_idx > 0))
    second_head = lax.broadcasted_iota(jnp.int32, (1, 2 * BLOCK), 1) >= BLOCK
    zeros_qt = jnp.zeros((HEAD_DIM, BLOCK), BF16)
    ones_rows = jnp.ones((2 * V7X_SUBLANES, 2 * BLOCK), BF16)

    def scores(j, head):
        c0 = j * BLOCK
        hkv = head // GROUP
        kb = k_ref[c0:c0 + 2 * BLOCK, :]
        cols = []
        for hd in (head, head + 1):
            qh = qt_ref[hd * HEAD_DIM:(hd + 1) * HEAD_DIM, c0:c0 + BLOCK]
            cols.append(jnp.concatenate(
                [qh, zeros_qt] if hkv == 0 else [zeros_qt, qh], axis=0))
        rhs = jnp.concatenate(cols, axis=1)
        return jnp.dot(kb, rhs, preferred_element_type=F32)

    def finish(j, head, st):
        c0 = j * BLOCK
        hkv = head // GROUP
        st = jnp.where(first_valid if j == 0 else band, st, NEG_MIN)
        sink = jnp.where(second_head, sink_ref[head + 1], sink_ref[head])
        m = jnp.maximum(jnp.max(st, axis=0, keepdims=True), sink)
        pt = jnp.exp(st - m).astype(BF16)
        vaug = jnp.concatenate(
            [vt_ref[hkv * HEAD_DIM:(hkv + 1) * HEAD_DIM, c0:c0 + 2 * BLOCK], ones_rows],
            axis=0)
        oa = jnp.dot(vaug, pt, preferred_element_type=F32)
        den = oa[HEAD_DIM:HEAD_DIM + 1, :] + jnp.exp(sink - m)
        ot = (oa[:HEAD_DIM, :] / den).astype(BF16)
        ot_ref[head * HEAD_DIM:(head + 1) * HEAD_DIM, c0:c0 + BLOCK] = ot[:, :BLOCK]
        ot_ref[(head + 1) * HEAD_DIM:(head + 2) * HEAD_DIM, c0:c0 + BLOCK] = ot[:, BLOCK:]

    work = [(j, head) for j in range(n_blocks) for head in range(0, N_HEADS, 2)]
    pending = [scores(*w) for w in work[:ATTN_LOOKAHEAD]]

    def attn_item(n):
        def run():
            if n + ATTN_LOOKAHEAD < len(work):
                pending.append(scores(*work[n + ATTN_LOOKAHEAD]))
            finish(*work[n], pending.pop(0))
        return run

    for item in _interleave(ffn_items, [attn_item(n) for n in range(len(work))]):
        item()

    k_ref[0:BLOCK, :] = k_ref[tq:tq + BLOCK, :]
    vt_ref[:, 0:BLOCK] = vt_ref[:, tq:tq + BLOCK]
    out = lax.dot_general(ot_ref[...], wo_ref[...], tn_dims, preferred_element_type=F32)
    mid_ref[...] = x_ref[0] + out + bo_ref[...]


def _conv_kernel(x_ref, g_ref, w1_ref, b1_ref, wdw_ref, bdw_ref, lng_ref, lnb_ref,
                 w2_ref, b2_ref, o_ref, u_ref):
    tc = x_ref.shape[1]
    n_chunks = tc // CONV_CHUNK
    first_tap_row = CONV_PAD - (CONV_WIDTH - 1)
    t_idx = pl.program_id(1)

    @pl.when(t_idx == 0)
    def _():
        u_ref[:, 0:CONV_PAD, :] = jnp.zeros((N_LANE_CHUNKS, CONV_PAD, V7X_LANES), F32)

    def glu_stage(r):
        lo = r * CONV_CHUNK
        h = _rmsnorm(x_ref[0, lo:lo + CONV_CHUNK, :], g_ref[...]).astype(BF16)
        a = _dot_padded(h, w1_ref) + b1_ref[...]
        u = a[:, :D_MODEL] * jax.nn.sigmoid(a[:, D_MODEL:])
        for l in range(N_LANE_CHUNKS):
            u_ref[l, CONV_PAD + lo:CONV_PAD + lo + CONV_CHUNK, :] = (
                u[:, l * V7X_LANES:(l + 1) * V7X_LANES])

    def conv_stage(r):
        lanes = []
        for l in range(N_LANE_CHUNKS):
            rows = []
            for s in range(CONV_CHUNK // CONV_ROWS):
                r0 = r * CONV_CHUNK + s * CONV_ROWS + first_tap_row
                acc = jnp.broadcast_to(bdw_ref[l], (CONV_ROWS, V7X_LANES))
                for j in range(CONV_WIDTH):
                    acc = acc + u_ref[l, r0 + j:r0 + j + CONV_ROWS, :] * wdw_ref[l, j:j + 1, :]
                rows.append(acc)
            lanes.append(jnp.concatenate(rows, axis=0))
        y = jnp.concatenate(lanes, axis=1)
        mu = jnp.mean(y, axis=-1, keepdims=True)
        yc = y - mu
        var = jnp.mean(yc * yc, axis=-1, keepdims=True)
        z = yc * lax.rsqrt(var + LN_EPS) * lng_ref[...] + lnb_ref[...]
        return (z * jax.nn.sigmoid(z)).astype(BF16)

    def out_stage(r, z):
        lo = r * CONV_CHUNK
        out = _dot_padded(z, w2_ref)
        o_ref[0, lo:lo + CONV_CHUNK, :] = x_ref[0, lo:lo + CONV_CHUNK, :] + out + b2_ref[...]

    glu_stage(0)
    for r in range(n_chunks):
        if r + 1 < n_chunks:
            glu_stage(r + 1)
        out_stage(r, conv_stage(r))

    for l in range(N_LANE_CHUNKS):
        u_ref[l, 0:CONV_PAD, :] = u_ref[l, tc:tc + CONV_PAD, :]


def _ffn_final_kernel(x_ref, g_ref, wup_ref, wdw_ref, bdw_ref, wdown_ref, gfin_ref,
                      o_ref, tail_ref, act_ref):
    @pl.when(pl.program_id(1) == 0)
    def _():
        tail_ref[...] = jnp.zeros(tail_ref.shape, F32)

    begin, items = _ffn_items(x_ref.at[0], g_ref, wup_ref, wdw_ref, bdw_ref, wdown_ref,
                              tail_ref, act_ref, o_ref)
    begin()
    for item in items:
        item()
    o_ref[0] = _rmsnorm(o_ref[0], gfin_ref[...])


def _resident(shape):
    zeros = (0,) * len(shape)
    return pl.BlockSpec(shape, lambda b, t: zeros, pipeline_mode=pl.Buffered(1))


class _Layer:
    def __init__(self, stacked, index):
        self.stacked = stacked
        self.index = index


def _operand(p):
    return p.stacked if isinstance(p, _Layer) else p


def _operand_spec(p):
    if not isinstance(p, _Layer):
        return _resident(p.shape)
    tail = (0,) * (p.stacked.ndim - 1)
    layer = p.index
    return pl.BlockSpec((None,) + tuple(p.stacked.shape[1:]), lambda b, t: (layer,) + tail,
                        pipeline_mode=pl.Buffered(1))


def _token_spec(tile=TOKEN_TILE):
    return pl.BlockSpec((1, tile, D_MODEL), lambda b, t: (b, t, 0))


def _skewed_specs(n_tiles):
    block = (1, TOKEN_TILE, D_MODEL)
    in_spec = pl.BlockSpec(block, lambda b, t: (b, jnp.minimum(t, n_tiles - 1), 0))
    out_spec = pl.BlockSpec(block, lambda b, t: (b, jnp.maximum(t - 1, 0), 0))
    return in_spec, out_spec


_COMPILER_PARAMS = pltpu.CompilerParams(
    dimension_semantics=("arbitrary", "arbitrary"),
    vmem_limit_bytes=V7X_VMEM_LIMIT_BYTES,
)


def _layer0(x, sinks, attn_params, ffn_params):
    bsz, seq, _ = x.shape
    n_tiles = seq // TOKEN_TILE
    in_spec, out_spec = _skewed_specs(n_tiles)
    params = tuple(attn_params) + tuple(ffn_params)
    return pl.pallas_call(
        _layer0_kernel,
        out_shape=jax.ShapeDtypeStruct(x.shape, F32),
        grid=(bsz, n_tiles + 1),
        in_specs=[pl.BlockSpec(memory_space=pltpu.SMEM), in_spec] + [
            _operand_spec(p) for p in params],
        out_specs=out_spec,
        scratch_shapes=[
            pltpu.VMEM((BLOCK + TOKEN_TILE, KV_DIM), BF16),
            pltpu.VMEM((KV_DIM, BLOCK + TOKEN_TILE), BF16),
            pltpu.VMEM((Q_DIM, TOKEN_TILE), BF16),
            pltpu.VMEM((Q_DIM, TOKEN_TILE), BF16),
            pltpu.VMEM((TOKEN_TILE, D_MODEL), F32),
            pltpu.VMEM((V7X_SUBLANES, D_FF), F32),
            pltpu.VMEM((TOKEN_TILE, D_FF), BF16),
        ],
        compiler_params=_COMPILER_PARAMS,
        name="layer0_swa_ffn",
    )(sinks, x, *[_operand(p) for p in params])


def _conv_layer(x, conv_params):
    bsz, seq, _ = x.shape
    return pl.pallas_call(
        _conv_kernel,
        out_shape=jax.ShapeDtypeStruct(x.shape, F32),
        grid=(bsz, seq // TOKEN_TILE),
        in_specs=[_token_spec()] + [_resident(a.shape) for a in conv_params],
        out_specs=_token_spec(),
        scratch_shapes=[
            pltpu.VMEM((N_LANE_CHUNKS, CONV_PAD + TOKEN_TILE, V7X_LANES), F32)],
        compiler_params=_COMPILER_PARAMS,
        name="conformer_conv",
    )(x, *conv_params)


def _ffn_final_layer(x, ffn_params, gfin):
    bsz, seq, _ = x.shape
    params = tuple(ffn_params) + (gfin,)
    return pl.pallas_call(
        _ffn_final_kernel,
        out_shape=jax.ShapeDtypeStruct(x.shape, F32),
        grid=(bsz, seq // FFN_FINAL_TILE),
        in_specs=[_token_spec(FFN_FINAL_TILE)] + [_operand_spec(p) for p in params],
        out_specs=_token_spec(FFN_FINAL_TILE),
        scratch_shapes=[
            pltpu.VMEM((V7X_SUBLANES, D_FF), F32),
            pltpu.VMEM((FFN_FINAL_TILE, D_FF), BF16),
        ],
        compiler_params=_COMPILER_PARAMS,
        name="conv_ffn_final",
    )(x, *[_operand(p) for p in params])


def _row(v):
    return v.reshape(1, -1).astype(F32)


def _ffn_params(i, norm_ffn, wup_all, ffn_w_dw, ffn_b_dw, wdown_all):
    return (_row(norm_ffn[i]), _Layer(wup_all, i), ffn_w_dw[i].astype(F32),
            _row(ffn_b_dw[i]), _Layer(wdown_all, i))


def kernel(x, norm_mix, attn_w_qkv, attn_b_qkv, attn_sinks, attn_w_o, attn_b_o,
           conv_w_pw1, conv_b_pw1, conv_w_dw, conv_b_dw, conv_ln_g, conv_ln_b,
           conv_w_pw2, conv_b_pw2, norm_ffn, ffn_w_up, ffn_w_dw, ffn_b_dw,
           ffn_w_down, final_norm):
    ffn = functools.partial(_ffn_params, norm_ffn=norm_ffn,
                            wup_all=ffn_w_up.astype(BF16), ffn_w_dw=ffn_w_dw,
                            ffn_b_dw=ffn_b_dw,
                            wdown_all=_pad_pitch(ffn_w_down.astype(BF16)))

    wqkv = attn_w_qkv[0]
    bqkv = attn_b_qkv[0].astype(F32)
    wqvt = jnp.concatenate([wqkv[:, :Q_DIM], wqkv[:, Q_DIM + KV_DIM:]], axis=1).T.astype(BF16)
    bqvt = jnp.concatenate([bqkv[:Q_DIM], bqkv[Q_DIM + KV_DIM:]])
    bqvt = jnp.broadcast_to(bqvt[:, None], (Q_DIM + KV_DIM, V7X_LANES))
    attn_params = (_row(norm_mix[0]), wqvt, bqvt,
                   wqkv[:, Q_DIM:Q_DIM + KV_DIM].astype(BF16),
                   _row(bqkv[Q_DIM:Q_DIM + KV_DIM]), attn_w_o[0].astype(BF16),
                   _row(attn_b_o[0]))
    x = _layer0(x, attn_sinks[0].astype(F32), attn_params, ffn(0))

    wdw = jnp.pad(conv_w_dw[0].astype(F32), ((0, CONV_PAD - CONV_WIDTH), (0, 0)))
    wdw = wdw.reshape(CONV_PAD, N_LANE_CHUNKS, V7X_LANES).transpose(1, 0, 2)
    bdw = conv_b_dw[0].astype(F32).reshape(N_LANE_CHUNKS, 1, V7X_LANES)
    conv_params = (_row(norm_mix[1]), _pad_pitch(conv_w_pw1[0].astype(BF16)),
                   _row(conv_b_pw1[0]), wdw, bdw, _row(conv_ln_g[0]), _row(conv_ln_b[0]),
                   _pad_pitch(conv_w_pw2[0].astype(BF16)), _row(conv_b_pw2[0]))
    x = _conv_layer(x, conv_params)
    return _ffn_final_layer(x, ffn(1), _row(final_norm))
```

```python
import functools
import math

import jax
import jax.numpy as jnp
from jax import lax
from jax.experimental import pallas as pl
from jax.experimental.pallas import tpu as pltpu

D_MODEL = 1024
N_HEADS = 16
N_KV_HEADS = 2
GROUP = N_HEADS // N_KV_HEADS
HEAD_DIM = 64
WINDOW = 128
BLOCK = 128
CONV_WIDTH = 31
D_FF = 2816
FFN_CONV_WIDTH = 3
RMS_EPS = 1e-6
LN_EPS = 1e-5

Q_DIM = N_HEADS * HEAD_DIM
KV_DIM = N_KV_HEADS * HEAD_DIM

V7X_LANES = 128
V7X_SUBLANES = 8
V7X_MXU_DIM = 256
V7X_VMEM_LIMIT_BYTES = 56 * 1024 * 1024

TOKEN_TILE = 512
FFN_FINAL_TILE = 1024
FFN_FINAL_ROWS = 256
CONV_TILE = 1024
ATTN_LOOKAHEAD = 3
FFN_CHUNK = V7X_MXU_DIM
CONV_CHUNK = 128
CONV_ROWS = 64
CONV_PAD = 32
N_LANE_CHUNKS = D_MODEL // V7X_LANES

F32 = jnp.float32
BF16 = jnp.bfloat16
NEG_MIN = float(jnp.finfo(jnp.float32).min)


def _rmsnorm(xf, g):
    ms = jnp.mean(xf * xf, axis=-1, keepdims=True)
    return xf * lax.rsqrt(ms + RMS_EPS) * g


def _pad_pitch(w):
    return jnp.concatenate([w, jnp.zeros(w.shape[:-1] + (V7X_LANES,), w.dtype)], axis=-1)


def _dot_padded(lhs, w_ref):
    return jnp.dot(lhs, w_ref[:, :w_ref.shape[1] - V7X_LANES], preferred_element_type=F32)


def _interleave(primary, secondary):
    merged = []
    done = 0
    for i, item in enumerate(primary):
        merged.append(item)
        target = ((i + 1) * len(secondary)) // len(primary)
        merged.extend(secondary[done:target])
        done = target
    return merged


def _ffn_items(xin_ref, g_ref, wup_ref, wdw_ref, bdw_ref, wdown_ref, tail_ref, act_ref,
               out_ref, h_ref=None, gfin_ref=None):
    tm = xin_ref.shape[0]
    state = {}

    def begin():
        if h_ref is not None:
            state["h"] = h_ref[...]
        else:
            state["h"] = _rmsnorm(xin_ref[...], g_ref[...]).astype(BF16)

    def up_item(lo):
        def run():
            hi = lo + FFN_CHUNK
            h = state["h"]
            gate = jnp.dot(h, wup_ref[:, lo:hi], preferred_element_type=F32)
            up = jnp.dot(h, wup_ref[:, D_FF + lo:D_FF + hi], preferred_element_type=F32)
            ext = jnp.concatenate([tail_ref[:, lo:hi], gate], axis=0)
            tail_ref[:, lo:hi] = gate[tm - V7X_SUBLANES:, :]
            conv = (gate * wdw_ref[2:3, lo:hi]
                    + ext[V7X_SUBLANES - 1:V7X_SUBLANES - 1 + tm] * wdw_ref[1:2, lo:hi]
                    + ext[V7X_SUBLANES - 2:V7X_SUBLANES - 2 + tm] * wdw_ref[0:1, lo:hi]
                    + bdw_ref[:, lo:hi])
            act_ref[:, lo:hi] = (conv * jax.nn.sigmoid(conv) * up).astype(BF16)
        return run

    def down_item(lo):
        def run():
            hi = lo + FFN_CHUNK
            out_ref[0, :, lo:hi] = xin_ref[:, lo:hi] + jnp.dot(
                act_ref[...], wdown_ref[:, lo:hi], preferred_element_type=F32)
        return run

    def down_rows_item(r0):
        def run():
            r1 = r0 + FFN_FINAL_ROWS
            out = xin_ref[r0:r1, :] + _dot_padded(act_ref[r0:r1, :], wdown_ref)
            out_ref[0, r0:r1, :] = _rmsnorm(out, gfin_ref[...])
        return run

    items = [up_item(lo) for lo in range(0, D_FF, FFN_CHUNK)]
    if gfin_ref is None:
        items += [down_item(lo) for lo in range(0, D_MODEL, FFN_CHUNK)]
    else:
        items += [down_rows_item(r0) for r0 in range(0, tm, FFN_FINAL_ROWS)]
    return begin, items


def _layer0_kernel(sink_ref, x_ref, ga_ref, wqvt_ref, bqvt_ref, wk_ref, bk_ref, wo_ref,
                   bo_ref, gf_ref, wup_ref, wdw_ref, bdw_ref, wdown_ref,
                   o_ref, k_ref, vt_ref, qt_ref, ot_ref, mid_ref, hmid_ref, tail_ref,
                   act_ref):
    tq = x_ref.shape[1]
    n_blocks = tq // BLOCK
    t_idx = pl.program_id(1)
    nt_dims = (((1,), (1,)), ((), ()))
    tn_dims = (((0,), (0,)), ((), ()))

    @pl.when(t_idx == 0)
    def _():
        k_ref[0:BLOCK, :] = jnp.zeros((BLOCK, KV_DIM), BF16)
        vt_ref[:, 0:BLOCK] = jnp.zeros((KV_DIM, BLOCK), BF16)
        mid_ref[...] = jnp.zeros(mid_ref.shape, F32)
        hmid_ref[...] = jnp.zeros(hmid_ref.shape, BF16)
        tail_ref[...] = jnp.zeros(tail_ref.shape, F32)

    ffn_begin, ffn_items = _ffn_items(mid_ref, gf_ref, wup_ref, wdw_ref, bdw_ref,
                                      wdown_ref, tail_ref, act_ref, o_ref, h_ref=hmid_ref)
    ffn_begin()
    ffn_items.pop(0)()

    half = tq // 2
    bias_t = jnp.concatenate([bqvt_ref[...]] * (half // V7X_LANES), axis=1)
    for c0 in range(0, tq, half):
        h = _rmsnorm(x_ref[0, c0:c0 + half, :], ga_ref[...]).astype(BF16)
        qvt = lax.dot_general(wqvt_ref[...], h, nt_dims, preferred_element_type=F32) + bias_t
        qt_ref[:, c0:c0 + half] = (qvt[:Q_DIM] * (1.0 / math.sqrt(HEAD_DIM))).astype(BF16)
        vt_ref[:, BLOCK + c0:BLOCK + c0 + half] = qvt[Q_DIM:].astype(BF16)
        kk = jnp.dot(h, wk_ref[...], preferred_element_type=F32) + bk_ref[...]
        k_ref[BLOCK + c0:BLOCK + c0 + half, :] = kk.astype(BF16)

    ki = lax.broadcasted_iota(jnp.int32, (2 * BLOCK, 2 * BLOCK), 0)
    qi = lax.broadcasted_iota(jnp.int32, (2 * BLOCK, 2 * BLOCK), 1) % BLOCK
    dist = qi + BLOCK - ki
    band = (dist >= 0) & (dist < WINDOW)
    first_valid = band & ((ki >= BLOCK) | (t_idx > 0))
    second_head = lax.broadcasted_iota(jnp.int32, (1, 2 * BLOCK), 1) >= BLOCK
    zeros_qt = jnp.zeros((HEAD_DIM, BLOCK), BF16)
    ones_rows = jnp.ones((2 * V7X_SUBLANES, 2 * BLOCK), BF16)

    def scores(j, head):
        c0 = j * BLOCK
        hkv = head // GROUP
        kb = k_ref[c0:c0 + 2 * BLOCK, :]
        cols = []
        for hd in (head, head + 1):
            qh = qt_ref[hd * HEAD_DIM:(hd + 1) * HEAD_DIM, c0:c0 + BLOCK]
            cols.append(jnp.concatenate(
                [qh, zeros_qt] if hkv == 0 else [zeros_qt, qh], axis=0))
        rhs = jnp.concatenate(cols, axis=1)
        return jnp.dot(kb, rhs, preferred_element_type=F32)

    def finish(j, head, st):
        c0 = j * BLOCK
        hkv = head // GROUP
        st = jnp.where(first_valid if j == 0 else band, st, NEG_MIN)
        sink = jnp.where(second_head, sink_ref[head + 1], sink_ref[head])
        m = jnp.maximum(jnp.max(st, axis=0, keepdims=True), sink)
        pt = jnp.exp(st - m).astype(BF16)
        vaug = jnp.concatenate(
            [vt_ref[hkv * HEAD_DIM:(hkv + 1) * HEAD_DIM, c0:c0 + 2 * BLOCK], ones_rows],
            axis=0)
        oa = jnp.dot(vaug, pt, preferred_element_type=F32)
        den = oa[HEAD_DIM:HEAD_DIM + 1, :] + jnp.exp(sink - m)
        ot = (oa[:HEAD_DIM, :] / den).astype(BF16)
        ot_ref[head * HEAD_DIM:(head + 1) * HEAD_DIM, c0:c0 + BLOCK] = ot[:, :BLOCK]
        ot_ref[(head + 1) * HEAD_DIM:(head + 2) * HEAD_DIM, c0:c0 + BLOCK] = ot[:, BLOCK:]

    work = [(j, head) for j in range(n_blocks) for head in range(0, N_HEADS, 2)]
    pending = [scores(*w) for w in work[:ATTN_LOOKAHEAD]]

    def attn_item(n):
        def run():
            if n + ATTN_LOOKAHEAD < len(work):
                pending.append(scores(*work[n + ATTN_LOOKAHEAD]))
            finish(*work[n], pending.pop(0))
        return run

    for item in _interleave(ffn_items, [attn_item(n) for n in range(len(work))]):
        item()

    k_ref[0:BLOCK, :] = k_ref[tq:tq + BLOCK, :]
    vt_ref[:, 0:BLOCK] = vt_ref[:, tq:tq + BLOCK]
    for c0 in range(0, tq, half):
        out = lax.dot_general(ot_ref[:, c0:c0 + half], wo_ref[...], tn_dims,
                              preferred_element_type=F32)
        mid = x_ref[0, c0:c0 + half, :] + out + bo_ref[...]
        mid_ref[c0:c0 + half, :] = mid
        hmid_ref[c0:c0 + half, :] = _rmsnorm(mid, gf_ref[...]).astype(BF16)


def _conv_kernel(x_ref, g_ref, w1_ref, b1_ref, wdw_ref, bdw_ref, lng_ref, lnb_ref,
                 w2_ref, b2_ref, o_ref, u_ref):
    tc = x_ref.shape[1]
    n_chunks = tc // CONV_CHUNK
    first_tap_row = CONV_PAD - (CONV_WIDTH - 1)
    t_idx = pl.program_id(1)

    @pl.when(t_idx == 0)
    def _():
        u_ref[:, 0:CONV_PAD, :] = jnp.zeros((N_LANE_CHUNKS, CONV_PAD, V7X_LANES), F32)

    def glu_stage(r):
        lo = r * CONV_CHUNK
        h = _rmsnorm(x_ref[0, lo:lo + CONV_CHUNK, :], g_ref[...]).astype(BF16)
        a = _dot_padded(h, w1_ref) + b1_ref[...]
        u = a[:, :D_MODEL] * jax.nn.sigmoid(a[:, D_MODEL:])
        for l in range(N_LANE_CHUNKS):
            u_ref[l, CONV_PAD + lo:CONV_PAD + lo + CONV_CHUNK, :] = (
                u[:, l * V7X_LANES:(l + 1) * V7X_LANES])

    def conv_stage(r):
        lanes = []
        for l in range(N_LANE_CHUNKS):
            rows = []
            for s in range(CONV_CHUNK // CONV_ROWS):
                r0 = r * CONV_CHUNK + s * CONV_ROWS + first_tap_row
                acc = jnp.broadcast_to(bdw_ref[l], (CONV_ROWS, V7X_LANES))
                for j in range(CONV_WIDTH):
                    acc = acc + u_ref[l, r0 + j:r0 + j + CONV_ROWS, :] * wdw_ref[l, j:j + 1, :]
                rows.append(acc)
            lanes.append(jnp.concatenate(rows, axis=0))
        y = jnp.concatenate(lanes, axis=1)
        mu = jnp.mean(y, axis=-1, keepdims=True)
        yc = y - mu
        var = jnp.mean(yc * yc, axis=-1, keepdims=True)
        z = yc * lax.rsqrt(var + LN_EPS) * lng_ref[...] + lnb_ref[...]
        return (z * jax.nn.sigmoid(z)).astype(BF16)

    def out_stage(r, z):
        lo = r * CONV_CHUNK
        out = _dot_padded(z, w2_ref)
        o_ref[0, lo:lo + CONV_CHUNK, :] = x_ref[0, lo:lo + CONV_CHUNK, :] + out + b2_ref[...]

    glu_stage(0)
    for r in range(n_chunks):
        if r + 1 < n_chunks:
            glu_stage(r + 1)
        out_stage(r, conv_stage(r))

    for l in range(N_LANE_CHUNKS):
        u_ref[l, 0:CONV_PAD, :] = u_ref[l, tc:tc + CONV_PAD, :]


def _ffn_final_kernel(x_ref, g_ref, wup_ref, wdw_ref, bdw_ref, wdown_ref, gfin_ref,
                      o_ref, tail_ref, act_ref):
    @pl.when(pl.program_id(1) == 0)
    def _():
        tail_ref[...] = jnp.zeros(tail_ref.shape, F32)

    begin, items = _ffn_items(x_ref.at[0], g_ref, wup_ref, wdw_ref, bdw_ref, wdown_ref,
                              tail_ref, act_ref, o_ref, gfin_ref=gfin_ref)
    begin()
    for item in items:
        item()


def _resident(shape):
    zeros = (0,) * len(shape)
    return pl.BlockSpec(shape, lambda b, t: zeros, pipeline_mode=pl.Buffered(1))


class _Layer:
    def __init__(self, stacked, index):
        self.stacked = stacked
        self.index = index


def _operand(p):
    return p.stacked if isinstance(p, _Layer) else p


def _operand_spec(p):
    if not isinstance(p, _Layer):
        return _resident(p.shape)
    tail = (0,) * (p.stacked.ndim - 1)
    layer = p.index
    return pl.BlockSpec((None,) + tuple(p.stacked.shape[1:]), lambda b, t: (layer,) + tail,
                        pipeline_mode=pl.Buffered(1))


def _token_spec(tile=TOKEN_TILE):
    return pl.BlockSpec((1, tile, D_MODEL), lambda b, t: (b, t, 0))


def _skewed_specs(n_tiles):
    block = (1, TOKEN_TILE, D_MODEL)
    in_spec = pl.BlockSpec(block, lambda b, t: (b, jnp.minimum(t, n_tiles - 1), 0))
    out_spec = pl.BlockSpec(block, lambda b, t: (b, jnp.maximum(t - 1, 0), 0))
    return in_spec, out_spec


_COMPILER_PARAMS = pltpu.CompilerParams(
    dimension_semantics=("arbitrary", "arbitrary"),
    vmem_limit_bytes=V7X_VMEM_LIMIT_BYTES,
)


def _layer0(x, sinks, attn_params, ffn_params):
    bsz, seq, _ = x.shape
    n_tiles = seq // TOKEN_TILE
    in_spec, out_spec = _skewed_specs(n_tiles)
    params = tuple(attn_params) + tuple(ffn_params)
    return pl.pallas_call(
        _layer0_kernel,
        out_shape=jax.ShapeDtypeStruct(x.shape, F32),
        grid=(bsz, n_tiles + 1),
        in_specs=[pl.BlockSpec(memory_space=pltpu.SMEM), in_spec] + [
            _operand_spec(p) for p in params],
        out_specs=out_spec,
        scratch_shapes=[
            pltpu.VMEM((BLOCK + TOKEN_TILE, KV_DIM), BF16),
            pltpu.VMEM((KV_DIM, BLOCK + TOKEN_TILE), BF16),
            pltpu.VMEM((Q_DIM, TOKEN_TILE), BF16),
            pltpu.VMEM((Q_DIM, TOKEN_TILE), BF16),
            pltpu.VMEM((TOKEN_TILE, D_MODEL), F32),
            pltpu.VMEM((TOKEN_TILE, D_MODEL), BF16),
            pltpu.VMEM((V7X_SUBLANES, D_FF), F32),
            pltpu.VMEM((TOKEN_TILE, D_FF), BF16),
        ],
        compiler_params=_COMPILER_PARAMS,
        name="layer0_swa_ffn",
    )(sinks, x, *[_operand(p) for p in params])


def _conv_layer(x, conv_params):
    bsz, seq, _ = x.shape
    return pl.pallas_call(
        _conv_kernel,
        out_shape=jax.ShapeDtypeStruct(x.shape, F32),
        grid=(bsz, seq // CONV_TILE),
        in_specs=[_token_spec(CONV_TILE)] + [_resident(a.shape) for a in conv_params],
        out_specs=_token_spec(CONV_TILE),
        scratch_shapes=[
            pltpu.VMEM((N_LANE_CHUNKS, CONV_PAD + CONV_TILE, V7X_LANES), F32)],
        compiler_params=_COMPILER_PARAMS,
        name="conformer_conv",
    )(x, *conv_params)


def _ffn_final_layer(x, ffn_params, gfin):
    bsz, seq, _ = x.shape
    params = tuple(ffn_params) + (gfin,)
    return pl.pallas_call(
        _ffn_final_kernel,
        out_shape=jax.ShapeDtypeStruct(x.shape, F32),
        grid=(bsz, seq // FFN_FINAL_TILE),
        in_specs=[_token_spec(FFN_FINAL_TILE)] + [_operand_spec(p) for p in params],
        out_specs=_token_spec(FFN_FINAL_TILE),
        scratch_shapes=[
            pltpu.VMEM((V7X_SUBLANES, D_FF), F32),
            pltpu.VMEM((FFN_FINAL_TILE, D_FF), BF16),
        ],
        compiler_params=_COMPILER_PARAMS,
        name="conv_ffn_final",
    )(x, *[_operand(p) for p in params])


def _row(v):
    return v.reshape(1, -1).astype(F32)


def _ffn_params(i, norm_ffn, wup_all, ffn_w_dw, ffn_b_dw, wdown_all):
    return (_row(norm_ffn[i]), _Layer(wup_all, i), ffn_w_dw[i].astype(F32),
            _row(ffn_b_dw[i]), _Layer(wdown_all, i))


def kernel(x, norm_mix, attn_w_qkv, attn_b_qkv, attn_sinks, attn_w_o, attn_b_o,
           conv_w_pw1, conv_b_pw1, conv_w_dw, conv_b_dw, conv_ln_g, conv_ln_b,
           conv_w_pw2, conv_b_pw2, norm_ffn, ffn_w_up, ffn_w_dw, ffn_b_dw,
           ffn_w_down, final_norm):
    ffn = functools.partial(_ffn_params, norm_ffn=norm_ffn,
                            wup_all=ffn_w_up.astype(BF16), ffn_w_dw=ffn_w_dw,
                            ffn_b_dw=ffn_b_dw,
                            wdown_all=_pad_pitch(ffn_w_down.astype(BF16)))

    wqkv = attn_w_qkv[0]
    bqkv = attn_b_qkv[0].astype(F32)
    wqvt = jnp.concatenate([wqkv[:, :Q_DIM], wqkv[:, Q_DIM + KV_DIM:]], axis=1).T.astype(BF16)
    bqvt = jnp.concatenate([bqkv[:Q_DIM], bqkv[Q_DIM + KV_DIM:]])
    bqvt = jnp.broadcast_to(bqvt[:, None], (Q_DIM + KV_DIM, V7X_LANES))
    attn_params = (_row(norm_mix[0]), wqvt, bqvt,
                   wqkv[:, Q_DIM:Q_DIM + KV_DIM].astype(BF16),
                   _row(bqkv[Q_DIM:Q_DIM + KV_DIM]), attn_w_o[0].astype(BF16),
                   _row(attn_b_o[0]))
    x = _layer0(x, attn_sinks[0].astype(F32), attn_params, ffn(0))

    wdw = jnp.pad(conv_w_dw[0].astype(F32), ((0, CONV_PAD - CONV_WIDTH), (0, 0)))
    wdw = wdw.reshape(CONV_PAD, N_LANE_CHUNKS, V7X_LANES).transpose(1, 0, 2)
    bdw = conv_b_dw[0].astype(F32).reshape(N_LANE_CHUNKS, 1, V7X_LANES)
    conv_params = (_row(norm_mix[1]), _pad_pitch(conv_w_pw1[0].astype(BF16)),
                   _row(conv_b_pw1[0]), wdw, bdw, _row(conv_ln_g[0]), _row(conv_ln_b[0]),
                   _pad_pitch(conv_w_pw2[0].astype(BF16)), _row(conv_b_pw2[0]))
    x = _conv_layer(x, conv_params)
    return _ffn_final_layer(x, ffn(1), _row(final_norm))
```

```python
import functools
import math

import jax
import jax.numpy as jnp
from jax import lax
from jax.experimental import pallas as pl
from jax.experimental.pallas import tpu as pltpu

D_MODEL = 1024
N_HEADS = 16
N_KV_HEADS = 2
GROUP = N_HEADS // N_KV_HEADS
HEAD_DIM = 64
WINDOW = 128
BLOCK = 128
CONV_WIDTH = 31
D_FF = 2816
FFN_CONV_WIDTH = 3
RMS_EPS = 1e-6
LN_EPS = 1e-5

Q_DIM = N_HEADS * HEAD_DIM
KV_DIM = N_KV_HEADS * HEAD_DIM

V7X_LANES = 128
V7X_SUBLANES = 8
V7X_MXU_DIM = 256
V7X_VMEM_LIMIT_BYTES = 56 * 1024 * 1024

TOKEN_TILE = 512
FFN_FINAL_TILE = 1024
FFN_FINAL_ROWS = 256
CONV_TILE = 1024
ATTN_LOOKAHEAD = 4
FFN_CHUNK = V7X_MXU_DIM
CONV_CHUNK = 128
CONV_ROWS = 64
CONV_PAD = 32
N_LANE_CHUNKS = D_MODEL // V7X_LANES

F32 = jnp.float32
BF16 = jnp.bfloat16
NEG_MIN = float(jnp.finfo(jnp.float32).min)


def _rmsnorm(xf, g):
    ms = jnp.mean(xf * xf, axis=-1, keepdims=True)
    return xf * lax.rsqrt(ms + RMS_EPS) * g


def _pad_pitch(w):
    return jnp.concatenate([w, jnp.zeros(w.shape[:-1] + (V7X_LANES,), w.dtype)], axis=-1)


def _dot_padded(lhs, w_ref):
    return jnp.dot(lhs, w_ref[:, :w_ref.shape[1] - V7X_LANES], preferred_element_type=F32)


def _interleave(primary, secondary):
    merged = []
    done = 0
    for i, item in enumerate(primary):
        merged.append(item)
        target = ((i + 1) * len(secondary)) // len(primary)
        merged.extend(secondary[done:target])
        done = target
    return merged


def _ffn_items(xin_ref, g_ref, wup_ref, wdw_ref, bdw_ref, wdown_ref, tail_ref, act_ref,
               out_ref, h_ref=None, gfin_ref=None):
    tm = xin_ref.shape[0]
    state = {}

    def begin():
        if h_ref is not None:
            state["h"] = h_ref[...]
        else:
            state["h"] = _rmsnorm(xin_ref[...], g_ref[...]).astype(BF16)

    def up_item(lo):
        def run():
            hi = lo + FFN_CHUNK
            h = state["h"]
            gate = jnp.dot(h, wup_ref[:, lo:hi], preferred_element_type=F32)
            up = jnp.dot(h, wup_ref[:, D_FF + lo:D_FF + hi], preferred_element_type=F32)
            ext = jnp.concatenate([tail_ref[:, lo:hi], gate], axis=0)
            tail_ref[:, lo:hi] = gate[tm - V7X_SUBLANES:, :]
            conv = (gate * wdw_ref[2:3, lo:hi]
                    + ext[V7X_SUBLANES - 1:V7X_SUBLANES - 1 + tm] * wdw_ref[1:2, lo:hi]
                    + ext[V7X_SUBLANES - 2:V7X_SUBLANES - 2 + tm] * wdw_ref[0:1, lo:hi]
                    + bdw_ref[:, lo:hi])
            act_ref[:, lo:hi] = (conv * jax.nn.sigmoid(conv) * up).astype(BF16)
        return run

    def down_item(lo):
        def run():
            hi = lo + FFN_CHUNK
            out_ref[0, :, lo:hi] = xin_ref[:, lo:hi] + jnp.dot(
                act_ref[...], wdown_ref[:, lo:hi], preferred_element_type=F32)
        return run

    def down_rows_item(r0):
        def run():
            r1 = r0 + FFN_FINAL_ROWS
            out = xin_ref[r0:r1, :] + jnp.dot(act_ref[r0:r1, :], wdown_ref[...],
                                              preferred_element_type=F32)
            out_ref[0, r0:r1, :] = _rmsnorm(out, gfin_ref[...])
        return run

    items = [up_item(lo) for lo in range(0, D_FF, FFN_CHUNK)]
    if gfin_ref is None:
        items += [down_item(lo) for lo in range(0, D_MODEL, FFN_CHUNK)]
    else:
        items += [down_rows_item(r0) for r0 in range(0, tm, FFN_FINAL_ROWS)]
    return begin, items


def _layer0_kernel(sink_ref, x_ref, ga_ref, wqvt_ref, bqvt_ref, wk_ref, bk_ref, wo_ref,
                   bo_ref, gf_ref, wup_ref, wdw_ref, bdw_ref, wdown_ref,
                   o_ref, k_ref, vt_ref, qt_ref, ot_ref, mid_ref, hmid_ref, tail_ref,
                   act_ref):
    tq = x_ref.shape[1]
    n_blocks = tq // BLOCK
    t_idx = pl.program_id(1)
    nt_dims = (((1,), (1,)), ((), ()))
    tn_dims = (((0,), (0,)), ((), ()))

    @pl.when(t_idx == 0)
    def _():
        k_ref[0:BLOCK, :] = jnp.zeros((BLOCK, KV_DIM), BF16)
        vt_ref[:, 0:BLOCK] = jnp.zeros((KV_DIM, BLOCK), BF16)
        mid_ref[...] = jnp.zeros(mid_ref.shape, F32)
        hmid_ref[...] = jnp.zeros(hmid_ref.shape, BF16)
        tail_ref[...] = jnp.zeros(tail_ref.shape, F32)

    ffn_begin, ffn_items = _ffn_items(mid_ref, gf_ref, wup_ref, wdw_ref, bdw_ref,
                                      wdown_ref, tail_ref, act_ref, o_ref, h_ref=hmid_ref)
    ffn_begin()
    ffn_items.pop(0)()

    half = tq // 2
    bias_t = jnp.concatenate([bqvt_ref[...]] * (half // V7X_LANES), axis=1)
    for c0 in range(0, tq, half):
        h = _rmsnorm(x_ref[0, c0:c0 + half, :], ga_ref[...]).astype(BF16)
        qvt = lax.dot_general(wqvt_ref[...], h, nt_dims, preferred_element_type=F32) + bias_t
        qt_ref[:, c0:c0 + half] = (qvt[:Q_DIM] * (1.0 / math.sqrt(HEAD_DIM))).astype(BF16)
        vt_ref[:, BLOCK + c0:BLOCK + c0 + half] = qvt[Q_DIM:].astype(BF16)
        kk = jnp.dot(h, wk_ref[...], preferred_element_type=F32) + bk_ref[...]
        k_ref[BLOCK + c0:BLOCK + c0 + half, :] = kk.astype(BF16)

    ki = lax.broadcasted_iota(jnp.int32, (2 * BLOCK, 2 * BLOCK), 0)
    qi = lax.broadcasted_iota(jnp.int32, (2 * BLOCK, 2 * BLOCK), 1) % BLOCK
    dist = qi + BLOCK - ki
    band = (dist >= 0) & (dist < WINDOW)
    first_valid = band & ((ki >= BLOCK) | (t_idx > 0))
    second_head = lax.broadcasted_iota(jnp.int32, (1, 2 * BLOCK), 1) >= BLOCK
    zeros_qt = jnp.zeros((HEAD_DIM, BLOCK), BF16)
    ones_rows = jnp.ones((2 * V7X_SUBLANES, 2 * BLOCK), BF16)

    def scores(j, head):
        c0 = j * BLOCK
        hkv = head // GROUP
        kb = k_ref[c0:c0 + 2 * BLOCK, :]
        cols = []
        for hd in (head, head + 1):
            qh = qt_ref[hd * HEAD_DIM:(hd + 1) * HEAD_DIM, c0:c0 + BLOCK]
            cols.append(jnp.concatenate(
                [qh, zeros_qt] if hkv == 0 else [zeros_qt, qh], axis=0))
        rhs = jnp.concatenate(cols, axis=1)
        return jnp.dot(kb, rhs, preferred_element_type=F32)

    def finish(j, head, st):
        c0 = j * BLOCK
        hkv = head // GROUP
        st = jnp.where(first_valid if j == 0 else band, st, NEG_MIN)
        sink = jnp.where(second_head, sink_ref[head + 1], sink_ref[head])
        m = jnp.maximum(jnp.max(st, axis=0, keepdims=True), sink)
        pt = jnp.exp(st - m).astype(BF16)
        vaug = jnp.concatenate(
            [vt_ref[hkv * HEAD_DIM:(hkv + 1) * HEAD_DIM, c0:c0 + 2 * BLOCK], ones_rows],
            axis=0)
        oa = jnp.dot(vaug, pt, preferred_element_type=F32)
        den = oa[HEAD_DIM:HEAD_DIM + 1, :] + jnp.exp(sink - m)
        ot = (oa[:HEAD_DIM, :] / den).astype(BF16)
        ot_ref[head * HEAD_DIM:(head + 1) * HEAD_DIM, c0:c0 + BLOCK] = ot[:, :BLOCK]
        ot_ref[(head + 1) * HEAD_DIM:(head + 2) * HEAD_DIM, c0:c0 + BLOCK] = ot[:, BLOCK:]

    work = [(j, head) for j in range(n_blocks) for head in range(0, N_HEADS, 2)]
    pending = [scores(*w) for w in work[:ATTN_LOOKAHEAD]]

    def attn_item(n):
        def run():
            if n + ATTN_LOOKAHEAD < len(work):
                pending.append(scores(*work[n + ATTN_LOOKAHEAD]))
            finish(*work[n], pending.pop(0))
        return run

    for item in _interleave(ffn_items, [attn_item(n) for n in range(len(work))]):
        item()

    k_ref[0:BLOCK, :] = k_ref[tq:tq + BLOCK, :]
    vt_ref[:, 0:BLOCK] = vt_ref[:, tq:tq + BLOCK]
    for c0 in range(0, tq, half):
        out = lax.dot_general(ot_ref[:, c0:c0 + half], wo_ref[...], tn_dims,
                              preferred_element_type=F32)
        mid = x_ref[0, c0:c0 + half, :] + out + bo_ref[...]
        mid_ref[c0:c0 + half, :] = mid
        hmid_ref[c0:c0 + half, :] = _rmsnorm(mid, gf_ref[...]).astype(BF16)


def _conv_kernel(x_ref, g_ref, w1_ref, b1_ref, wdw_ref, bdw_ref, lng_ref, lnb_ref,
                 w2_ref, b2_ref, o_ref, u_ref):
    tc = x_ref.shape[1]
    n_chunks = tc // CONV_CHUNK
    first_tap_row = CONV_PAD - (CONV_WIDTH - 1)
    t_idx = pl.program_id(1)

    @pl.when(t_idx == 0)
    def _():
        u_ref[:, 0:CONV_PAD, :] = jnp.zeros((N_LANE_CHUNKS, CONV_PAD, V7X_LANES), F32)

    def glu_stage(r):
        lo = r * CONV_CHUNK
        h = _rmsnorm(x_ref[0, lo:lo + CONV_CHUNK, :], g_ref[...]).astype(BF16)
        a = _dot_padded(h, w1_ref) + b1_ref[...]
        u = a[:, :D_MODEL] * jax.nn.sigmoid(a[:, D_MODEL:])
        for l in range(N_LANE_CHUNKS):
            u_ref[l, CONV_PAD + lo:CONV_PAD + lo + CONV_CHUNK, :] = (
                u[:, l * V7X_LANES:(l + 1) * V7X_LANES])

    def conv_stage(r):
        lanes = []
        for l in range(N_LANE_CHUNKS):
            rows = []
            for s in range(CONV_CHUNK // CONV_ROWS):
                r0 = r * CONV_CHUNK + s * CONV_ROWS + first_tap_row
                acc = jnp.broadcast_to(bdw_ref[l], (CONV_ROWS, V7X_LANES))
                for j in range(CONV_WIDTH):
                    acc = acc + u_ref[l, r0 + j:r0 + j + CONV_ROWS, :] * wdw_ref[l, j:j + 1, :]
                rows.append(acc)
            lanes.append(jnp.concatenate(rows, axis=0))
        y = jnp.concatenate(lanes, axis=1)
        mu = jnp.mean(y, axis=-1, keepdims=True)
        yc = y - mu
        var = jnp.mean(yc * yc, axis=-1, keepdims=True)
        z = yc * lax.rsqrt(var + LN_EPS) * lng_ref[...] + lnb_ref[...]
        return (z * jax.nn.sigmoid(z)).astype(BF16)

    def out_stage(r, z):
        lo = r * CONV_CHUNK
        out = _dot_padded(z, w2_ref)
        o_ref[0, lo:lo + CONV_CHUNK, :] = x_ref[0, lo:lo + CONV_CHUNK, :] + out + b2_ref[...]

    glu_stage(0)
    for r in range(n_chunks):
        if r + 1 < n_chunks:
            glu_stage(r + 1)
        out_stage(r, conv_stage(r))

    for l in range(N_LANE_CHUNKS):
        u_ref[l, 0:CONV_PAD, :] = u_ref[l, tc:tc + CONV_PAD, :]


def _ffn_final_kernel(x_ref, g_ref, wup_ref, wdw_ref, bdw_ref, wdown_ref, gfin_ref,
                      o_ref, tail_ref, act_ref):
    @pl.when(pl.program_id(1) == 0)
    def _():
        tail_ref[...] = jnp.zeros(tail_ref.shape, F32)

    begin, items = _ffn_items(x_ref.at[0], g_ref, wup_ref, wdw_ref, bdw_ref, wdown_ref,
                              tail_ref, act_ref, o_ref, gfin_ref=gfin_ref)
    begin()
    for item in items:
        item()


def _resident(shape):
    zeros = (0,) * len(shape)
    return pl.BlockSpec(shape, lambda b, t: zeros, pipeline_mode=pl.Buffered(1))


class _Layer:
    def __init__(self, stacked, index):
        self.stacked = stacked
        self.index = index


def _operand(p):
    return p.stacked if isinstance(p, _Layer) else p


def _operand_spec(p):
    if not isinstance(p, _Layer):
        return _resident(p.shape)
    tail = (0,) * (p.stacked.ndim - 1)
    layer = p.index
    return pl.BlockSpec((None,) + tuple(p.stacked.shape[1:]), lambda b, t: (layer,) + tail,
                        pipeline_mode=pl.Buffered(1))


def _token_spec(tile=TOKEN_TILE):
    return pl.BlockSpec((1, tile, D_MODEL), lambda b, t: (b, t, 0))


def _skewed_specs(n_tiles):
    block = (1, TOKEN_TILE, D_MODEL)
    in_spec = pl.BlockSpec(block, lambda b, t: (b, jnp.minimum(t, n_tiles - 1), 0))
    out_spec = pl.BlockSpec(block, lambda b, t: (b, jnp.maximum(t - 1, 0), 0))
    return in_spec, out_spec


_COMPILER_PARAMS = pltpu.CompilerParams(
    dimension_semantics=("arbitrary", "arbitrary"),
    vmem_limit_bytes=V7X_VMEM_LIMIT_BYTES,
)


def _layer0(x, sinks, attn_params, ffn_params):
    bsz, seq, _ = x.shape
    n_tiles = seq // TOKEN_TILE
    in_spec, out_spec = _skewed_specs(n_tiles)
    params = tuple(attn_params) + tuple(ffn_params)
    return pl.pallas_call(
        _layer0_kernel,
        out_shape=jax.ShapeDtypeStruct(x.shape, F32),
        grid=(bsz, n_tiles + 1),
        in_specs=[pl.BlockSpec(memory_space=pltpu.SMEM), in_spec] + [
            _operand_spec(p) for p in params],
        out_specs=out_spec,
        scratch_shapes=[
            pltpu.VMEM((BLOCK + TOKEN_TILE, KV_DIM), BF16),
            pltpu.VMEM((KV_DIM, BLOCK + TOKEN_TILE), BF16),
            pltpu.VMEM((Q_DIM, TOKEN_TILE), BF16),
            pltpu.VMEM((Q_DIM, TOKEN_TILE), BF16),
            pltpu.VMEM((TOKEN_TILE, D_MODEL), F32),
            pltpu.VMEM((TOKEN_TILE, D_MODEL), BF16),
            pltpu.VMEM((V7X_SUBLANES, D_FF), F32),
            pltpu.VMEM((TOKEN_TILE, D_FF), BF16),
        ],
        compiler_params=_COMPILER_PARAMS,
        name="layer0_swa_ffn",
    )(sinks, x, *[_operand(p) for p in params])


def _conv_layer(x, conv_params):
    bsz, seq, _ = x.shape
    return pl.pallas_call(
        _conv_kernel,
        out_shape=jax.ShapeDtypeStruct(x.shape, F32),
        grid=(bsz, seq // CONV_TILE),
        in_specs=[_token_spec(CONV_TILE)] + [_resident(a.shape) for a in conv_params],
        out_specs=_token_spec(CONV_TILE),
        scratch_shapes=[
            pltpu.VMEM((N_LANE_CHUNKS, CONV_PAD + CONV_TILE, V7X_LANES), F32)],
        compiler_params=_COMPILER_PARAMS,
        name="conformer_conv",
    )(x, *conv_params)


def _ffn_final_layer(x, ffn_params, gfin):
    bsz, seq, _ = x.shape
    params = tuple(ffn_params) + (gfin,)
    return pl.pallas_call(
        _ffn_final_kernel,
        out_shape=jax.ShapeDtypeStruct(x.shape, F32),
        grid=(bsz, seq // FFN_FINAL_TILE),
        in_specs=[_token_spec(FFN_FINAL_TILE)] + [_operand_spec(p) for p in params],
        out_specs=_token_spec(FFN_FINAL_TILE),
        scratch_shapes=[
            pltpu.VMEM((V7X_SUBLANES, D_FF), F32),
            pltpu.VMEM((FFN_FINAL_TILE, D_FF), BF16),
        ],
        compiler_params=_COMPILER_PARAMS,
        name="conv_ffn_final",
    )(x, *[_operand(p) for p in params])


def _row(v):
    return v.reshape(1, -1).astype(F32)


def _ffn_params(i, norm_ffn, wup_all, ffn_w_dw, ffn_b_dw, wdown_all):
    return (_row(norm_ffn[i]), _Layer(wup_all, i), ffn_w_dw[i].astype(F32),
            _row(ffn_b_dw[i]), _Layer(wdown_all, i))


def kernel(x, norm_mix, attn_w_qkv, attn_b_qkv, attn_sinks, attn_w_o, attn_b_o,
           conv_w_pw1, conv_b_pw1, conv_w_dw, conv_b_dw, conv_ln_g, conv_ln_b,
           conv_w_pw2, conv_b_pw2, norm_ffn, ffn_w_up, ffn_w_dw, ffn_b_dw,
           ffn_w_down, final_norm):
    ffn = functools.partial(_ffn_params, norm_ffn=norm_ffn,
                            wup_all=ffn_w_up.astype(BF16), ffn_w_dw=ffn_w_dw,
                            ffn_b_dw=ffn_b_dw,
                            wdown_all=ffn_w_down.astype(BF16))

    wqkv = attn_w_qkv[0]
    bqkv = attn_b_qkv[0].astype(F32)
    wqvt = jnp.concatenate([wqkv[:, :Q_DIM], wqkv[:, Q_DIM + KV_DIM:]], axis=1).T.astype(BF16)
    bqvt = jnp.concatenate([bqkv[:Q_DIM], bqkv[Q_DIM + KV_DIM:]])
    bqvt = jnp.broadcast_to(bqvt[:, None], (Q_DIM + KV_DIM, V7X_LANES))
    attn_params = (_row(norm_mix[0]), wqvt, bqvt,
                   wqkv[:, Q_DIM:Q_DIM + KV_DIM].astype(BF16),
                   _row(bqkv[Q_DIM:Q_DIM + KV_DIM]), attn_w_o[0].astype(BF16),
                   _row(attn_b_o[0]))
    x = _layer0(x, attn_sinks[0].astype(F32), attn_params, ffn(0))

    wdw = jnp.pad(conv_w_dw[0].astype(F32), ((0, CONV_PAD - CONV_WIDTH), (0, 0)))
    wdw = wdw.reshape(CONV_PAD, N_LANE_CHUNKS, V7X_LANES).transpose(1, 0, 2)
    bdw = conv_b_dw[0].astype(F32).reshape(N_LANE_CHUNKS, 1, V7X_LANES)
    conv_params = (_row(norm_mix[1]), _pad_pitch(conv_w_pw1[0].astype(BF16)),
                   _row(conv_b_pw1[0]), wdw, bdw, _row(conv_ln_g[0]), _row(conv_ln_b[0]),
                   _pad_pitch(conv_w_pw2[0].astype(BF16)), _row(conv_b_pw2[0]))
    x = _conv_layer(x, conv_params)
    return _ffn_final_layer(x, ffn(1), _row(final_norm))
```

```python
import functools
import math

import jax
import jax.numpy as jnp
from jax import lax
from jax.experimental import pallas as pl
from jax.experimental.pallas import tpu as pltpu

D_MODEL = 1024
N_HEADS = 16
N_KV_HEADS = 2
GROUP = N_HEADS // N_KV_HEADS
HEAD_DIM = 64
WINDOW = 128
BLOCK = 128
CONV_WIDTH = 31
D_FF = 2816
FFN_CONV_WIDTH = 3
RMS_EPS = 1e-6
LN_EPS = 1e-5

Q_DIM = N_HEADS * HEAD_DIM
KV_DIM = N_KV_HEADS * HEAD_DIM

V7X_LANES = 128
V7X_SUBLANES = 8
V7X_MXU_DIM = 256
V7X_VMEM_LIMIT_BYTES = 56 * 1024 * 1024

TOKEN_TILE = 512
FFN_FINAL_TILE = 1024
FFN_FINAL_ROWS = 256
CONV_TILE = 1024
ATTN_LOOKAHEAD = 4
FFN_CHUNK = V7X_MXU_DIM
CONV_CHUNK = 128
CONV_ROWS = 64
CONV_PAD = 32
N_LANE_CHUNKS = D_MODEL // V7X_LANES

F32 = jnp.float32
BF16 = jnp.bfloat16
NEG_MIN = float(jnp.finfo(jnp.float32).min)


def _rmsnorm(xf, g):
    ms = jnp.mean(xf * xf, axis=-1, keepdims=True)
    return xf * lax.rsqrt(ms + RMS_EPS) * g


def _pad_pitch(w):
    return jnp.concatenate([w, jnp.zeros(w.shape[:-1] + (V7X_LANES,), w.dtype)], axis=-1)


def _dot_padded(lhs, w_ref):
    return jnp.dot(lhs, w_ref[:, :w_ref.shape[1] - V7X_LANES], preferred_element_type=F32)


def _interleave(primary, secondary):
    merged = []
    done = 0
    for i, item in enumerate(primary):
        merged.append(item)
        target = ((i + 1) * len(secondary)) // len(primary)
        merged.extend(secondary[done:target])
        done = target
    return merged


def _ffn_items(xin_ref, g_ref, wup_ref, wdw_ref, bdw_ref, wdown_ref, tail_ref, act_ref,
               out_ref, h_ref=None, gfin_ref=None):
    tm = xin_ref.shape[0]
    state = {}

    def begin():
        if h_ref is not None:
            state["h_parts"] = [h_ref[...]]
        else:
            state["h_parts"] = [
                _rmsnorm(xin_ref[r0:r0 + FFN_FINAL_ROWS, :], g_ref[...]).astype(BF16)
                for r0 in range(0, tm, FFN_FINAL_ROWS)]
        state["h"] = jnp.concatenate(state["h_parts"], axis=0)

    def up_dot(lo, first):
        w = wup_ref[:, lo:lo + FFN_CHUNK]
        if not first:
            return jnp.dot(state["h"], w, preferred_element_type=F32)
        return jnp.concatenate(
            [jnp.dot(hp, w, preferred_element_type=F32) for hp in state["h_parts"]], axis=0)

    def up_item(lo):
        def run():
            hi = lo + FFN_CHUNK
            gate = up_dot(lo, first=(lo == 0))
            up = up_dot(D_FF + lo, first=(lo == 0))
            ext = jnp.concatenate([tail_ref[:, lo:hi], gate], axis=0)
            tail_ref[:, lo:hi] = gate[tm - V7X_SUBLANES:, :]
            conv = (gate * wdw_ref[2:3, lo:hi]
                    + ext[V7X_SUBLANES - 1:V7X_SUBLANES - 1 + tm] * wdw_ref[1:2, lo:hi]
                    + ext[V7X_SUBLANES - 2:V7X_SUBLANES - 2 + tm] * wdw_ref[0:1, lo:hi]
                    + bdw_ref[:, lo:hi])
            act_ref[:, lo:hi] = (conv * jax.nn.sigmoid(conv) * up).astype(BF16)
        return run

    def down_item(lo):
        def run():
            hi = lo + FFN_CHUNK
            out_ref[0, :, lo:hi] = xin_ref[:, lo:hi] + jnp.dot(
                act_ref[...], wdown_ref[:, lo:hi], preferred_element_type=F32)
        return run

    def down_rows_item(r0):
        def run():
            r1 = r0 + FFN_FINAL_ROWS
            out = xin_ref[r0:r1, :] + _dot_padded(act_ref[r0:r1, :], wdown_ref)
            out_ref[0, r0:r1, :] = _rmsnorm(out, gfin_ref[...])
        return run

    items = [up_item(lo) for lo in range(0, D_FF, FFN_CHUNK)]
    if gfin_ref is None:
        items += [down_item(lo) for lo in range(0, D_MODEL, FFN_CHUNK)]
    else:
        items += [down_rows_item(r0) for r0 in range(0, tm, FFN_FINAL_ROWS)]
    return begin, items


def _layer0_kernel(sink_ref, x_ref, ga_ref, wqvt_ref, bqvt_ref, wk_ref, bk_ref, wo_ref,
                   bo_ref, gf_ref, wup_ref, wdw_ref, bdw_ref, wdown_ref,
                   o_ref, k_ref, vt_ref, qt_ref, ot_ref, mid_ref, hmid_ref, tail_ref,
                   act_ref):
    tq = x_ref.shape[1]
    n_blocks = tq // BLOCK
    t_idx = pl.program_id(1)
    nt_dims = (((1,), (1,)), ((), ()))
    tn_dims = (((0,), (0,)), ((), ()))

    @pl.when(t_idx == 0)
    def _():
        k_ref[0:BLOCK, :] = jnp.zeros((BLOCK, KV_DIM), BF16)
        vt_ref[:, 0:BLOCK] = jnp.zeros((KV_DIM, BLOCK), BF16)
        mid_ref[...] = jnp.zeros(mid_ref.shape, F32)
        hmid_ref[...] = jnp.zeros(hmid_ref.shape, BF16)
        tail_ref[...] = jnp.zeros(tail_ref.shape, F32)

    ffn_begin, ffn_items = _ffn_items(mid_ref, gf_ref, wup_ref, wdw_ref, bdw_ref,
                                      wdown_ref, tail_ref, act_ref, o_ref, h_ref=hmid_ref)
    ffn_begin()
    ffn_items.pop(0)()

    half = tq // 2
    bias_t = jnp.concatenate([bqvt_ref[...]] * (half // V7X_LANES), axis=1)
    for c0 in range(0, tq, half):
        h = _rmsnorm(x_ref[0, c0:c0 + half, :], ga_ref[...]).astype(BF16)
        qvt = lax.dot_general(wqvt_ref[...], h, nt_dims, preferred_element_type=F32) + bias_t
        qt_ref[:, c0:c0 + half] = (qvt[:Q_DIM] * (1.0 / math.sqrt(HEAD_DIM))).astype(BF16)
        vt_ref[:, BLOCK + c0:BLOCK + c0 + half] = qvt[Q_DIM:].astype(BF16)
        kk = jnp.dot(h, wk_ref[...], preferred_element_type=F32) + bk_ref[...]
        k_ref[BLOCK + c0:BLOCK + c0 + half, :] = kk.astype(BF16)

    ki = lax.broadcasted_iota(jnp.int32, (2 * BLOCK, 2 * BLOCK), 0)
    qi = lax.broadcasted_iota(jnp.int32, (2 * BLOCK, 2 * BLOCK), 1) % BLOCK
    dist = qi + BLOCK - ki
    band = (dist >= 0) & (dist < WINDOW)
    first_valid = band & ((ki >= BLOCK) | (t_idx > 0))
    second_head = lax.broadcasted_iota(jnp.int32, (1, 2 * BLOCK), 1) >= BLOCK
    zeros_qt = jnp.zeros((HEAD_DIM, BLOCK), BF16)
    ones_rows = jnp.ones((2 * V7X_SUBLANES, 2 * BLOCK), BF16)

    def scores(j, head):
        c0 = j * BLOCK
        hkv = head // GROUP
        kb = k_ref[c0:c0 + 2 * BLOCK, :]
        cols = []
        for hd in (head, head + 1):
            qh = qt_ref[hd * HEAD_DIM:(hd + 1) * HEAD_DIM, c0:c0 + BLOCK]
            cols.append(jnp.concatenate(
                [qh, zeros_qt] if hkv == 0 else [zeros_qt, qh], axis=0))
        rhs = jnp.concatenate(cols, axis=1)
        return jnp.dot(kb, rhs, preferred_element_type=F32)

    def finish(j, head, st):
        c0 = j * BLOCK
        hkv = head // GROUP
        st = jnp.where(first_valid if j == 0 else band, st, NEG_MIN)
        sink = jnp.where(second_head, sink_ref[head + 1], sink_ref[head])
        m = jnp.maximum(jnp.max(st, axis=0, keepdims=True), sink)
        pt = jnp.exp(st - m).astype(BF16)
        vaug = jnp.concatenate(
            [vt_ref[hkv * HEAD_DIM:(hkv + 1) * HEAD_DIM, c0:c0 + 2 * BLOCK], ones_rows],
            axis=0)
        oa = jnp.dot(vaug, pt, preferred_element_type=F32)
        den = oa[HEAD_DIM:HEAD_DIM + 1, :] + jnp.exp(sink - m)
        ot = (oa[:HEAD_DIM, :] / den).astype(BF16)
        ot_ref[head * HEAD_DIM:(head + 1) * HEAD_DIM, c0:c0 + BLOCK] = ot[:, :BLOCK]
        ot_ref[(head + 1) * HEAD_DIM:(head + 2) * HEAD_DIM, c0:c0 + BLOCK] = ot[:, BLOCK:]

    work = [(j, head) for j in range(n_blocks) for head in range(0, N_HEADS, 2)]
    pending = [scores(*w) for w in work[:ATTN_LOOKAHEAD]]

    def attn_item(n):
        def run():
            if n + ATTN_LOOKAHEAD < len(work):
                pending.append(scores(*work[n + ATTN_LOOKAHEAD]))
            finish(*work[n], pending.pop(0))
        return run

    for item in _interleave(ffn_items, [attn_item(n) for n in range(len(work))]):
        item()

    k_ref[0:BLOCK, :] = k_ref[tq:tq + BLOCK, :]
    vt_ref[:, 0:BLOCK] = vt_ref[:, tq:tq + BLOCK]
    for c0 in range(0, tq, half):
        out = lax.dot_general(ot_ref[:, c0:c0 + half], wo_ref[...], tn_dims,
                              preferred_element_type=F32)
        mid = x_ref[0, c0:c0 + half, :] + out + bo_ref[...]
        mid_ref[c0:c0 + half, :] = mid
        hmid_ref[c0:c0 + half, :] = _rmsnorm(mid, gf_ref[...]).astype(BF16)


def _conv_kernel(x_ref, g_ref, w1_ref, b1_ref, wdw_ref, bdw_ref, lng_ref, lnb_ref,
                 w2_ref, b2_ref, o_ref, u_ref):
    tc = x_ref.shape[1]
    n_chunks = tc // CONV_CHUNK
    first_tap_row = CONV_PAD - (CONV_WIDTH - 1)
    t_idx = pl.program_id(1)

    @pl.when(t_idx == 0)
    def _():
        u_ref[:, 0:CONV_PAD, :] = jnp.zeros((N_LANE_CHUNKS, CONV_PAD, V7X_LANES), F32)

    def glu_stage(r):
        lo = r * CONV_CHUNK
        h = _rmsnorm(x_ref[0, lo:lo + CONV_CHUNK, :], g_ref[...]).astype(BF16)
        a = _dot_padded(h, w1_ref) + b1_ref[...]
        u = a[:, :D_MODEL] * jax.nn.sigmoid(a[:, D_MODEL:])
        for l in range(N_LANE_CHUNKS):
            u_ref[l, CONV_PAD + lo:CONV_PAD + lo + CONV_CHUNK, :] = (
                u[:, l * V7X_LANES:(l + 1) * V7X_LANES])

    def conv_stage(r):
        lanes = []
        for l in range(N_LANE_CHUNKS):
            rows = []
            for s in range(CONV_CHUNK // CONV_ROWS):
                r0 = r * CONV_CHUNK + s * CONV_ROWS + first_tap_row
                acc = jnp.broadcast_to(bdw_ref[l], (CONV_ROWS, V7X_LANES))
                for j in range(CONV_WIDTH):
                    acc = acc + u_ref[l, r0 + j:r0 + j + CONV_ROWS, :] * wdw_ref[l, j:j + 1, :]
                rows.append(acc)
            lanes.append(jnp.concatenate(rows, axis=0))
        y = jnp.concatenate(lanes, axis=1)
        mu = jnp.mean(y, axis=-1, keepdims=True)
        yc = y - mu
        var = jnp.mean(yc * yc, axis=-1, keepdims=True)
        z = yc * lax.rsqrt(var + LN_EPS) * lng_ref[...] + lnb_ref[...]
        return (z * jax.nn.sigmoid(z)).astype(BF16)

    def out_stage(r, z):
        lo = r * CONV_CHUNK
        out = _dot_padded(z, w2_ref)
        o_ref[0, lo:lo + CONV_CHUNK, :] = x_ref[0, lo:lo + CONV_CHUNK, :] + out + b2_ref[...]

    glu_stage(0)
    for r in range(n_chunks):
        if r + 1 < n_chunks:
            glu_stage(r + 1)
        out_stage(r, conv_stage(r))

    for l in range(N_LANE_CHUNKS):
        u_ref[l, 0:CONV_PAD, :] = u_ref[l, tc:tc + CONV_PAD, :]


def _ffn_final_kernel(x_ref, g_ref, wup_ref, wdw_ref, bdw_ref, wdown_ref, gfin_ref,
                      o_ref, tail_ref, act_ref):
    @pl.when(pl.program_id(1) == 0)
    def _():
        tail_ref[...] = jnp.zeros(tail_ref.shape, F32)

    begin, items = _ffn_items(x_ref.at[0], g_ref, wup_ref, wdw_ref, bdw_ref, wdown_ref,
                              tail_ref, act_ref, o_ref, gfin_ref=gfin_ref)
    begin()
    for item in items:
        item()


def _resident(shape):
    zeros = (0,) * len(shape)
    return pl.BlockSpec(shape, lambda b, t: zeros, pipeline_mode=pl.Buffered(1))


class _Layer:
    def __init__(self, stacked, index):
        self.stacked = stacked
        self.index = index


def _operand(p):
    return p.stacked if isinstance(p, _Layer) else p


def _operand_spec(p):
    if not isinstance(p, _Layer):
        return _resident(p.shape)
    tail = (0,) * (p.stacked.ndim - 1)
    layer = p.index
    return pl.BlockSpec((None,) + tuple(p.stacked.shape[1:]), lambda b, t: (layer,) + tail,
                        pipeline_mode=pl.Buffered(1))


def _token_spec(tile=TOKEN_TILE):
    return pl.BlockSpec((1, tile, D_MODEL), lambda b, t: (b, t, 0))


def _skewed_specs(n_tiles):
    block = (1, TOKEN_TILE, D_MODEL)
    in_spec = pl.BlockSpec(block, lambda b, t: (b, jnp.minimum(t, n_tiles - 1), 0))
    out_spec = pl.BlockSpec(block, lambda b, t: (b, jnp.maximum(t - 1, 0), 0))
    return in_spec, out_spec


_COMPILER_PARAMS = pltpu.CompilerParams(
    dimension_semantics=("arbitrary", "arbitrary"),
    vmem_limit_bytes=V7X_VMEM_LIMIT_BYTES,
)


def _layer0(x, sinks, attn_params, ffn_params):
    bsz, seq, _ = x.shape
    n_tiles = seq // TOKEN_TILE
    in_spec, out_spec = _skewed_specs(n_tiles)
    params = tuple(attn_params) + tuple(ffn_params)
    return pl.pallas_call(
        _layer0_kernel,
        out_shape=jax.ShapeDtypeStruct(x.shape, F32),
        grid=(bsz, n_tiles + 1),
        in_specs=[pl.BlockSpec(memory_space=pltpu.SMEM), in_spec] + [
            _operand_spec(p) for p in params],
        out_specs=out_spec,
        scratch_shapes=[
            pltpu.VMEM((BLOCK + TOKEN_TILE, KV_DIM), BF16),
            pltpu.VMEM((KV_DIM, BLOCK + TOKEN_TILE), BF16),
            pltpu.VMEM((Q_DIM, TOKEN_TILE), BF16),
            pltpu.VMEM((Q_DIM, TOKEN_TILE), BF16),
            pltpu.VMEM((TOKEN_TILE, D_MODEL), F32),
            pltpu.VMEM((TOKEN_TILE, D_MODEL), BF16),
            pltpu.VMEM((V7X_SUBLANES, D_FF), F32),
            pltpu.VMEM((TOKEN_TILE, D_FF), BF16),
        ],
        compiler_params=_COMPILER_PARAMS,
        name="layer0_swa_ffn",
    )(sinks, x, *[_operand(p) for p in params])


def _conv_layer(x, conv_params):
    bsz, seq, _ = x.shape
    return pl.pallas_call(
        _conv_kernel,
        out_shape=jax.ShapeDtypeStruct(x.shape, F32),
        grid=(bsz, seq // CONV_TILE),
        in_specs=[_token_spec(CONV_TILE)] + [_resident(a.shape) for a in conv_params],
        out_specs=_token_spec(CONV_TILE),
        scratch_shapes=[
            pltpu.VMEM((N_LANE_CHUNKS, CONV_PAD + CONV_TILE, V7X_LANES), F32)],
        compiler_params=_COMPILER_PARAMS,
        name="conformer_conv",
    )(x, *conv_params)


def _ffn_final_layer(x, ffn_params, gfin):
    bsz, seq, _ = x.shape
    params = tuple(ffn_params) + (gfin,)
    return pl.pallas_call(
        _ffn_final_kernel,
        out_shape=jax.ShapeDtypeStruct(x.shape, F32),
        grid=(bsz, seq // FFN_FINAL_TILE),
        in_specs=[_token_spec(FFN_FINAL_TILE)] + [_operand_spec(p) for p in params],
        out_specs=_token_spec(FFN_FINAL_TILE),
        scratch_shapes=[
            pltpu.VMEM((V7X_SUBLANES, D_FF), F32),
            pltpu.VMEM((FFN_FINAL_TILE, D_FF), BF16),
        ],
        compiler_params=_COMPILER_PARAMS,
        name="conv_ffn_final",
    )(x, *[_operand(p) for p in params])


def _row(v):
    return v.reshape(1, -1).astype(F32)


def _ffn_params(i, norm_ffn, wup_all, ffn_w_dw, ffn_b_dw, wdown_all):
    return (_row(norm_ffn[i]), _Layer(wup_all, i), ffn_w_dw[i].astype(F32),
            _row(ffn_b_dw[i]), _Layer(wdown_all, i))


def kernel(x, norm_mix, attn_w_qkv, attn_b_qkv, attn_sinks, attn_w_o, attn_b_o,
           conv_w_pw1, conv_b_pw1, conv_w_dw, conv_b_dw, conv_ln_g, conv_ln_b,
           conv_w_pw2, conv_b_pw2, norm_ffn, ffn_w_up, ffn_w_dw, ffn_b_dw,
           ffn_w_down, final_norm):
    ffn = functools.partial(_ffn_params, norm_ffn=norm_ffn,
                            wup_all=ffn_w_up.astype(BF16), ffn_w_dw=ffn_w_dw,
                            ffn_b_dw=ffn_b_dw,
                            wdown_all=_pad_pitch(ffn_w_down.astype(BF16)))

    wqkv = attn_w_qkv[0]
    bqkv = attn_b_qkv[0].astype(F32)
    wqvt = jnp.concatenate([wqkv[:, :Q_DIM], wqkv[:, Q_DIM + KV_DIM:]], axis=1).T.astype(BF16)
    bqvt = jnp.concatenate([bqkv[:Q_DIM], bqkv[Q_DIM + KV_DIM:]])
    bqvt = jnp.broadcast_to(bqvt[:, None], (Q_DIM + KV_DIM, V7X_LANES))
    attn_params = (_row(norm_mix[0]), wqvt, bqvt,
                   wqkv[:, Q_DIM:Q_DIM + KV_DIM].astype(BF16),
                   _row(bqkv[Q_DIM:Q_DIM + KV_DIM]), attn_w_o[0].astype(BF16),
                   _row(attn_b_o[0]))
    x = _layer0(x, attn_sinks[0].astype(F32), attn_params, ffn(0))

    wdw = jnp.pad(conv_w_dw[0].astype(F32), ((0, CONV_PAD - CONV_WIDTH), (0, 0)))
    wdw = wdw.reshape(CONV_PAD, N_LANE_CHUNKS, V7X_LANES).transpose(1, 0, 2)
    bdw = conv_b_dw[0].astype(F32).reshape(N_LANE_CHUNKS, 1, V7X_LANES)
    conv_params = (_row(norm_mix[1]), _pad_pitch(conv_w_pw1[0].astype(BF16)),
                   _row(conv_b_pw1[0]), wdw, bdw, _row(conv_ln_g[0]), _row(conv_ln_b[0]),
                   _pad_pitch(conv_w_pw2[0].astype(BF16)), _row(conv_b_pw2[0]))
    x = _conv_layer(x, conv_params)
    return _ffn_final_layer(x, ffn(1), _row(final_norm))
```
